```python
import jax, jax.numpy as jnp
from jax import lax
import numpy as np

D_MODEL = 1024
BATCH = 8
SEQ = 2048
DEPTH = 4
DEC_BATCH = 128
DEC_SEQ = 4
PAST_LEN = 16384
PAGE_SIZE = 128

D_CONV = D_MODEL // 2
CONV_WIDTH = 31
D_POOL = D_MODEL // 2
POOL_WINDOWS = (2, 4, 8, 16)
POOL_GROUPS = len(POOL_WINDOWS)
POOL_GC = D_POOL // POOL_GROUPS
POOL_BUF = max(POOL_WINDOWS) - 1
D_SGU = D_MODEL // 2
SGU_GROUPS = 4
SGU_GC = D_SGU // SGU_GROUPS
CHUNK = 128
N_BRANCH = 3
D_FF = 2816
D_IN = 2 * D_CONV + D_POOL + 2 * D_SGU + N_BRANCH * D_MODEL
IN_SPLITS = (2 * D_CONV, 2 * D_CONV + D_POOL, 2 * D_CONV + D_POOL + D_SGU, 2 * D_CONV + D_POOL + 2 * D_SGU)
RMS_EPS = 1e-6
LN_EPS = 1e-5

kernel_name = "hybrid_conv_pool_sgu_decoder_step"


def _rmsnorm(x, g):
    x32 = x.astype(jnp.float32)
    y = x32 * lax.rsqrt(jnp.mean(x32 * x32, axis=-1, keepdims=True) + RMS_EPS)
    return (y * g.astype(jnp.float32)).astype(x.dtype)


def _layernorm(x, g, b):
    x32 = x.astype(jnp.float32)
    mu = jnp.mean(x32, axis=-1, keepdims=True)
    var = jnp.mean(jnp.square(x32 - mu), axis=-1, keepdims=True)
    y = (x32 - mu) * lax.rsqrt(var + LN_EPS) * g.astype(jnp.float32) + b.astype(jnp.float32)
    return y.astype(x.dtype)


def _swiglu_ffn(x, g, w_gu, w_d):
    h = _rmsnorm(x, g)
    gate, up = jnp.split(h @ w_gu, 2, axis=-1)
    return (jax.nn.silu(gate) * up) @ w_d


def _conv_module(a2, buf, dw_w, dw_b, ln_g, ln_b, w_out):
    a_val, a_gate = jnp.split(a2, 2, axis=-1)
    a = a_val * jax.nn.sigmoid(a_gate)
    padded = jnp.concatenate([buf.astype(a.dtype), a], axis=1)
    new_buf = padded[:, -(CONV_WIDTH - 1):]
    y = lax.conv_general_dilated(
        padded, dw_w[:, None, :].astype(a.dtype), window_strides=(1,), padding='VALID',
        dimension_numbers=('NWC', 'WIO', 'NWC'), feature_group_count=D_CONV)
    y = _layernorm(y + dw_b, ln_g, ln_b)
    return jax.nn.silu(y) @ w_out, new_buf


def _pool_mixer(p, buf, start, pool_w, pool_scale, w_out):
    bsz, L, _ = p.shape
    padded = jnp.concatenate([buf.astype(p.dtype), p], axis=1)
    new_buf = padded[:, -POOL_BUF:]
    cs = jnp.cumsum(padded.astype(jnp.float32), axis=1)
    cs = jnp.concatenate([jnp.zeros((bsz, 1, D_POOL), jnp.float32), cs], axis=1)
    pos = start + jnp.arange(L, dtype=jnp.int32)
    hi = cs[:, POOL_BUF + 1:POOL_BUF + 1 + L]
    means = []
    for g, w in enumerate(POOL_WINDOWS):
        sl = slice(g * POOL_GC, (g + 1) * POOL_GC)
        lo = cs[:, POOL_BUF + 1 - w:POOL_BUF + 1 - w + L, sl]
        cnt = jnp.minimum(pos + 1, w).astype(jnp.float32)[None, :, None]
        means.append((hi[..., sl] - lo) / cnt)
    m = (jnp.concatenate(means, axis=-1) - p.astype(jnp.float32)).astype(p.dtype)
    m = m.reshape(bsz, L, POOL_GROUPS, POOL_GC)
    y = jnp.einsum('blgc,gcd->blgd', m, pool_w).reshape(bsz, L, D_POOL)
    return (y * pool_scale) @ w_out, new_buf


def _sgu_mixer(u, v, ln_g, ln_b, w_s, b_s, w_out):
    bsz, L, _ = v.shape
    vn = _layernorm(v, ln_g, ln_b)
    n_chunks = -(-L // CHUNK)
    pad = n_chunks * CHUNK - L
    vp = jnp.pad(vn, ((0, 0), (0, pad), (0, 0))).reshape(bsz, n_chunks, CHUNK, SGU_GROUPS, SGU_GC)
    w_causal = jnp.tril(w_s)
    mixed = jnp.einsum('gts,bnsgc->bntgc', w_causal, vp) + jnp.transpose(b_s)[None, None, :, :, None]
    mixed = mixed.reshape(bsz, n_chunks * CHUNK, D_SGU)[:, :L]
    return (u * mixed) @ w_out, vn


def _layer(x, conv_buf, pool_buf, start, ffn1_norm, ffn1_w_gate_up, ffn1_w_down, mix_norm, w_in,
           conv_dw_w, conv_dw_b, conv_ln_g, conv_ln_b, w_conv_out, pool_w, pool_scale, w_pool_out,
           sgu_ln_g, sgu_ln_b, sgu_w, sgu_b, w_sgu_out, w_o, ffn2_norm, ffn2_w_gate_up, ffn2_w_down):
    x = x + 0.5 * _swiglu_ffn(x, ffn1_norm, ffn1_w_gate_up, ffn1_w_down)
    h = _rmsnorm(x, mix_norm)
    z = h @ w_in
    a2, p, u, v, gl = jnp.split(z, IN_SPLITS, axis=-1)
    y_a, conv_buf = _conv_module(a2, conv_buf, conv_dw_w, conv_dw_b, conv_ln_g, conv_ln_b, w_conv_out)
    y_b, pool_buf = _pool_mixer(p, pool_buf, start, pool_w, pool_scale, w_pool_out)
    y_c, vn = _sgu_mixer(u, v, sgu_ln_g, sgu_ln_b, sgu_w, sgu_b, w_sgu_out)
    g_a, g_b, g_c = jnp.split(jax.nn.sigmoid(gl), N_BRANCH, axis=-1)
    x = x + (g_a * y_a + g_b * y_b + g_c * y_c) @ w_o
    x = x + 0.5 * _swiglu_ffn(x, ffn2_norm, ffn2_w_gate_up, ffn2_w_down)
    return x, conv_buf, pool_buf, vn


def _trunk(x, conv_state, pool_state, start, layer_params, final_norm):
    conv_out, pool_out, v_out = [], [], []
    for l in range(DEPTH):
        x, cb, pb, vn = _layer(x, conv_state[l], pool_state[l], start, *[prm[l] for prm in layer_params])
        conv_out.append(cb)
        pool_out.append(pb)
        v_out.append(vn)
    return _rmsnorm(x, final_norm), jnp.stack(conv_out), jnp.stack(pool_out), jnp.stack(v_out)


def setup_inputs(seed: int = 0) -> dict:
    key = jax.random.key(seed)
    ks = iter(jax.random.split(key, 40))

    def nrm(shape, scale):
        return jax.random.normal(next(ks), shape, jnp.float32) * scale

    return {
        "x_prompt": nrm((BATCH, SEQ, D_MODEL), 1.0),
        "x_sample": nrm((DEC_BATCH, DEC_SEQ, D_MODEL), 1.0),
        "state_conv": nrm((DEPTH, DEC_BATCH, CONV_WIDTH - 1, D_CONV), 0.5),
        "state_pool": nrm((DEPTH, DEC_BATCH, POOL_BUF, D_POOL), 0.6),
        "ffn1_norm": 1.0 + nrm((DEPTH, D_MODEL), 0.05),
        "ffn1_w_gate_up": nrm((DEPTH, D_MODEL, 2 * D_FF), D_MODEL ** -0.5),
        "ffn1_w_down": nrm((DEPTH, D_FF, D_MODEL), D_FF ** -0.5),
        "mix_norm": 1.0 + nrm((DEPTH, D_MODEL), 0.05),
        "w_in": nrm((DEPTH, D_MODEL, D_IN), D_MODEL ** -0.5),
        "conv_dw_w": nrm((DEPTH, CONV_WIDTH, D_CONV), CONV_WIDTH ** -0.5),
        "conv_dw_b": nrm((DEPTH, D_CONV), 0.02),
        "conv_ln_g": 1.0 + nrm((DEPTH, D_CONV), 0.05),
        "conv_ln_b": nrm((DEPTH, D_CONV), 0.02),
        "w_conv_out": nrm((DEPTH, D_CONV, D_MODEL), D_CONV ** -0.5),
        "pool_w": nrm((DEPTH, POOL_GROUPS, POOL_GC, POOL_GC), POOL_GC ** -0.5),
        "pool_scale": 1.0 + nrm((DEPTH, D_POOL), 0.1),
        "w_pool_out": nrm((DEPTH, D_POOL, D_MODEL), D_POOL ** -0.5),
        "sgu_ln_g": 1.0 + nrm((DEPTH, D_SGU), 0.05),
        "sgu_ln_b": nrm((DEPTH, D_SGU), 0.02),
        "sgu_w": nrm((DEPTH, SGU_GROUPS, CHUNK, CHUNK), 0.5 * CHUNK ** -0.5),
        "sgu_b": 1.0 + nrm((DEPTH, SGU_GROUPS, CHUNK), 0.05),
        "w_sgu_out": nrm((DEPTH, D_SGU, D_MODEL), D_SGU ** -0.5),
        "w_o": nrm((DEPTH, D_MODEL, D_MODEL), D_MODEL ** -0.5),
        "ffn2_norm": 1.0 + nrm((DEPTH, D_MODEL), 0.05),
        "ffn2_w_gate_up": nrm((DEPTH, D_MODEL, 2 * D_FF), D_MODEL ** -0.5),
        "ffn2_w_down": nrm((DEPTH, D_FF, D_MODEL), D_FF ** -0.5),
        "final_norm": 1.0 + nrm((D_MODEL,), 0.05),
    }


def reference(x_prompt, x_sample, state_conv, state_pool, ffn1_norm, ffn1_w_gate_up, ffn1_w_down, mix_norm, w_in,
              conv_dw_w, conv_dw_b, conv_ln_g, conv_ln_b, w_conv_out, pool_w, pool_scale, w_pool_out,
              sgu_ln_g, sgu_ln_b, sgu_w, sgu_b, w_sgu_out, w_o, ffn2_norm, ffn2_w_gate_up, ffn2_w_down, final_norm):
    layer_params = (ffn1_norm, ffn1_w_gate_up, ffn1_w_down, mix_norm, w_in,
                    conv_dw_w, conv_dw_b, conv_ln_g, conv_ln_b, w_conv_out, pool_w, pool_scale, w_pool_out,
                    sgu_ln_g, sgu_ln_b, sgu_w, sgu_b, w_sgu_out, w_o, ffn2_norm, ffn2_w_gate_up, ffn2_w_down)
    bsz = x_prompt.shape[0]
    zero_conv = jnp.zeros((DEPTH, bsz, CONV_WIDTH - 1, D_CONV), x_prompt.dtype)
    zero_pool = jnp.zeros((DEPTH, bsz, POOL_BUF, D_POOL), x_prompt.dtype)
    y_prompt, conv_prompt, pool_prompt, _ = _trunk(x_prompt, zero_conv, zero_pool, 0, layer_params, final_norm)
    y_sample, conv_sample, pool_sample, chunk_v_sample = _trunk(x_sample, state_conv, state_pool, PAST_LEN,
                                                                layer_params, final_norm)
    return (y_prompt, y_sample, conv_prompt, conv_sample, pool_prompt, pool_sample, chunk_v_sample)
```

```python
import functools

import jax
import jax.numpy as jnp
from jax import lax
from jax.experimental import pallas as pl
from jax.experimental.pallas import tpu as pltpu

D_MODEL = 1024
BATCH = 8
SEQ = 2048
DEPTH = 4
DEC_BATCH = 128
DEC_SEQ = 4
PAST_LEN = 16384
D_CONV = 512
CONV_WIDTH = 31
CONV_BUF = CONV_WIDTH - 1
D_POOL = 512
POOL_WINDOWS = (2, 4, 8, 16)
POOL_BUF = 15
D_SGU = 512
SGU_GROUPS = 4
CHUNK = 128
D_FF = 2816
D_IN = 5632
RMS_EPS = 1e-6
LN_EPS = 1e-5

LANES = 128
N_SLAB = D_CONV // LANES
TM = 512
CONV_HALO = 32
POOL_HALO = 16
ROW_BLK = 64
HALF_BLK = ROW_BLK // 2
FF_CHUNKS = ((0, 1536), (1536, 1280))
VMEM_LIMIT = 56 * 1024 * 1024

_BF = jnp.bfloat16
_F32 = jnp.float32


def _dot(a, b):
    return jnp.dot(a, b, preferred_element_type=_F32)


def _rmsnorm(x, g):
    return x * lax.rsqrt(jnp.mean(x * x, axis=-1, keepdims=True) + RMS_EPS) * g


def _layernorm(x, g, b):
    mu = jnp.mean(x, axis=-1, keepdims=True)
    xc = x - mu
    var = jnp.mean(xc * xc, axis=-1, keepdims=True)
    return xc * lax.rsqrt(var + LN_EPS) * g + b


def _silu(x):
    return x * jax.nn.sigmoid(x)


def _ffn_body(x_ref, g_ref, wgu_ref, wd_ref, fn_ref, o_ref, *, final):
    x = x_ref[...]
    h = _rmsnorm(x, g_ref[...]).astype(_BF)
    acc = None
    for c0, cw in FF_CHUNKS:
        gate = _dot(h, wgu_ref[:, c0:c0 + cw])
        up = _dot(h, wgu_ref[:, D_FF + c0:D_FF + c0 + cw])
        part = _dot((_silu(gate) * up).astype(_BF), wd_ref[c0:c0 + cw, :])
        acc = part if acc is None else acc + part
    y = x + 0.5 * acc
    if final:
        y = _rmsnorm(y, fn_ref[...])
    o_ref[...] = y


def _layer_spec(tail, layer, grid_rank):
    zeros = (0,) * len(tail)
    return pl.BlockSpec((None,) + tuple(tail), lambda *_: (layer,) + zeros,
                        pipeline_mode=pl.Buffered(1))


def _ffn(x, norm, wgu, wd, final_norm, layer, final):
    n_tok = x.shape[0]
    return pl.pallas_call(
        functools.partial(_ffn_body, final=final),
        grid=(n_tok // TM,),
        in_specs=[
            pl.BlockSpec((TM, D_MODEL), lambda i: (i, 0)),
            _layer_spec((1, D_MODEL), layer, 1),
            _layer_spec((D_MODEL, 2 * D_FF), layer, 1),
            _layer_spec((D_FF, D_MODEL), layer, 1),
            pl.BlockSpec((1, D_MODEL), lambda i: (0, 0)),
        ],
        out_specs=pl.BlockSpec((TM, D_MODEL), lambda i: (i, 0)),
        out_shape=jax.ShapeDtypeStruct((n_tok, D_MODEL), _F32),
        compiler_params=pltpu.CompilerParams(
            dimension_semantics=("arbitrary",), vmem_limit_bytes=VMEM_LIMIT),
        name="ffn",
    )(x, norm, wgu, wd, final_norm)


def _pool_linear(m, pw_ref, ps_ref, wpo_ref):
    y = jnp.concatenate(
        [_dot(m[:, g * LANES:(g + 1) * LANES].astype(_BF), pw_ref[g]) for g in range(len(POOL_WINDOWS))],
        axis=1)
    return _dot((y * ps_ref[...]).astype(_BF), wpo_ref[...])


def _merge_out(x, h, win_ref, wo_ref, y_a, y_b, y_c):
    gl = jax.nn.sigmoid(_dot(h, win_ref[:, 2560:5632]))
    merged = (gl[:, 0:D_MODEL] * y_a + gl[:, D_MODEL:2 * D_MODEL] * y_b
              + gl[:, 2 * D_MODEL:3 * D_MODEL] * y_c)
    return x + _dot(merged.astype(_BF), wo_ref[...])


def _mix_prompt_body(x_ref, g_ref, win_ref, cw_ref, cb_ref, clg_ref, clb_ref, wco_ref,
                     pw_ref, ps_ref, wpo_ref, slg_ref, slb_ref, sw_ref, sb_ref, wso_ref, wo_ref,
                     xo_ref, convo_ref, poolo_ref,
                     ah_ref, ph_ref, yc_ref, m_ref):
    s = pl.program_id(1)
    n_s = pl.num_programs(1)

    @pl.when(s == 0)
    def _():
        for g in range(N_SLAB):
            ah_ref[g, 0:CONV_HALO, :] = jnp.zeros((CONV_HALO, LANES), _F32)
            ph_ref[g, 0:POOL_HALO, :] = jnp.zeros((POOL_HALO, LANES), _F32)

    x = x_ref[...]
    h = _rmsnorm(x, g_ref[...]).astype(_BF)

    a2 = _dot(h, win_ref[:, 0:2 * D_CONV])
    a = a2[:, 0:D_CONV] * jax.nn.sigmoid(a2[:, D_CONV:2 * D_CONV])
    for g in range(N_SLAB):
        ah_ref[g, CONV_HALO:CONV_HALO + TM, :] = a[:, g * LANES:(g + 1) * LANES]

    def conv_step(i, carry):
        g = i // (TM // ROW_BLK)
        base = (i % (TM // ROW_BLK)) * ROW_BLK
        for ph in range(2):
            acc = jnp.zeros((HALF_BLK, LANES), _F32)
            for k in range(CONV_WIDTH):
                start = base + (CONV_HALO - CONV_BUF) + k + ph
                acc = acc + cw_ref[g, k:k + 1, :] * ah_ref[g, pl.ds(start, HALF_BLK, stride=2), :]
            yc_ref[g, pl.ds(base + ph, HALF_BLK, stride=2), :] = acc
        return carry

    lax.fori_loop(0, N_SLAB * (TM // ROW_BLK), conv_step, 0)

    @pl.when(s == n_s - 1)
    def _():
        for g in range(N_SLAB):
            convo_ref[:, g * LANES:(g + 1) * LANES] = ah_ref[g, CONV_HALO + TM - CONV_BUF:CONV_HALO + TM, :]

    for g in range(N_SLAB):
        ah_ref[g, 0:CONV_HALO, :] = ah_ref[g, TM:TM + CONV_HALO, :]

    yc = jnp.concatenate([yc_ref[g] for g in range(N_SLAB)], axis=1) + cb_ref[...]
    yc = _silu(_layernorm(yc, clg_ref[...], clb_ref[...]))
    y_a = _dot(yc.astype(_BF), wco_ref[...])

    p = _dot(h, win_ref[:, 1024:1536])
    for g in range(N_SLAB):
        ph_ref[g, POOL_HALO:POOL_HALO + TM, :] = p[:, g * LANES:(g + 1) * LANES]

    def pool_step(i, carry):
        base = i * ROW_BLK
        for ph in range(2):
            pos = s * TM + base + ph + 2 * lax.broadcasted_iota(jnp.int32, (HALF_BLK, LANES), 0)
            for g, w in enumerate(POOL_WINDOWS):
                cur = ph_ref[g, pl.ds(base + POOL_HALO + ph, HALF_BLK, stride=2), :]
                tot = cur
                for j in range(1, w):
                    tot = tot + ph_ref[g, pl.ds(base + POOL_HALO + ph - j, HALF_BLK, stride=2), :]
                cnt = jnp.minimum(pos + 1, w).astype(_F32)
                m_ref[g, pl.ds(base + ph, HALF_BLK, stride=2), :] = tot / cnt - cur
        return carry

    lax.fori_loop(0, TM // ROW_BLK, pool_step, 0)

    @pl.when(s == n_s - 1)
    def _():
        for g in range(N_SLAB):
            poolo_ref[:, g * LANES:(g + 1) * LANES] = ph_ref[g, POOL_HALO + TM - POOL_BUF:POOL_HALO + TM, :]

    for g in range(N_SLAB):
        ph_ref[g, 0:POOL_HALO, :] = ph_ref[g, TM:TM + POOL_HALO, :]

    m = jnp.concatenate([m_ref[g] for g in range(N_SLAB)], axis=1)
    y_b = _pool_linear(m, pw_ref, ps_ref, wpo_ref)

    u = _dot(h, win_ref[:, 1536:2048])
    v = _dot(h, win_ref[:, 2048:2560])
    vn = _layernorm(v, slg_ref[...], slb_ref[...]).astype(_BF)
    n_chunk = TM // CHUNK
    row = lax.broadcasted_iota(jnp.int32, (CHUNK, CHUNK), 0)
    col = lax.broadcasted_iota(jnp.int32, (CHUNK, CHUNK), 1)
    mixed_groups = []
    for g in range(SGU_GROUPS):
        w_causal = jnp.where(row >= col, sw_ref[g], 0.0).astype(_BF)
        vg = vn[:, g * LANES:(g + 1) * LANES]
        rhs = jnp.concatenate([vg[c * CHUNK:(c + 1) * CHUNK, :] for c in range(n_chunk)], axis=1)
        res = _dot(w_causal, rhs)
        mixed_groups.append(
            jnp.concatenate([res[:, c * LANES:(c + 1) * LANES] for c in range(n_chunk)], axis=0))
    bias = jnp.concatenate([sb_ref[...]] * n_chunk, axis=0)
    mixed = jnp.concatenate(mixed_groups, axis=1) + bias
    y_c = _dot((u * mixed).astype(_BF), wso_ref[...])

    xo_ref[...] = _merge_out(x, h, win_ref, wo_ref, y_a, y_b, y_c)


def _mix_prompt(x, layer, prm):
    n_s = SEQ // TM
    tok = pl.BlockSpec((TM, D_MODEL), lambda b, s: (b * n_s + s, 0))
    ls = lambda *tail: _layer_spec(tail, layer, 2)
    in_specs = [
        tok,
        ls(1, D_MODEL),
        ls(D_MODEL, D_IN),
        ls(N_SLAB, CONV_WIDTH, LANES),
        ls(1, D_CONV), ls(1, D_CONV), ls(1, D_CONV),
        ls(D_CONV, D_MODEL),
        ls(len(POOL_WINDOWS), LANES, LANES),
        ls(1, D_POOL),
        ls(D_POOL, D_MODEL),
        ls(1, D_SGU), ls(1, D_SGU),
        ls(SGU_GROUPS, CHUNK, CHUNK),
        ls(CHUNK, D_SGU),
        ls(D_SGU, D_MODEL),
        ls(D_MODEL, D_MODEL),
    ]
    out_specs = [
        tok,
        pl.BlockSpec((None, CONV_BUF, D_CONV), lambda b, s: (b, 0, 0)),
        pl.BlockSpec((None, POOL_BUF, D_POOL), lambda b, s: (b, 0, 0)),
    ]
    out_shape = [
        jax.ShapeDtypeStruct((BATCH * SEQ, D_MODEL), _F32),
        jax.ShapeDtypeStruct((BATCH, CONV_BUF, D_CONV), _F32),
        jax.ShapeDtypeStruct((BATCH, POOL_BUF, D_POOL), _F32),
    ]
    scratch = [
        pltpu.VMEM((N_SLAB, CONV_HALO + TM, LANES), _F32),
        pltpu.VMEM((N_SLAB, POOL_HALO + TM, LANES), _F32),
        pltpu.VMEM((N_SLAB, TM, LANES), _F32),
        pltpu.VMEM((N_SLAB, TM, LANES), _F32),
    ]
    return pl.pallas_call(
        _mix_prompt_body,
        grid=(BATCH, n_s),
        in_specs=in_specs, out_specs=out_specs, out_shape=out_shape,
        scratch_shapes=scratch,
        compiler_params=pltpu.CompilerParams(
            dimension_semantics=("arbitrary", "arbitrary"), vmem_limit_bytes=VMEM_LIMIT),
        name="mix_prompt",
    )(x, *prm)


def _mix_sample_body(x_ref, sc_ref, sp_ref, g_ref, win_ref, cw_ref, cb_ref, clg_ref, clb_ref, wco_ref,
                     pw_ref, ps_ref, wpo_ref, slg_ref, slb_ref, sw4_ref, sb4_ref, wso_ref, wo_ref,
                     xo_ref, convo_ref, poolo_ref, vn_ref):
    nb = DEC_BATCH
    x = x_ref[...]
    h = _rmsnorm(x, g_ref[...]).astype(_BF)

    a2 = _dot(h, win_ref[:, 0:2 * D_CONV])
    a = a2[:, 0:D_CONV] * jax.nn.sigmoid(a2[:, D_CONV:2 * D_CONV])
    a_t = [a[t * nb:(t + 1) * nb, :] for t in range(DEC_SEQ)]
    padded = lambda j: sc_ref[j] if j < CONV_BUF else a_t[j - CONV_BUF]
    ys = []
    for t in range(DEC_SEQ):
        acc = jnp.zeros((nb, D_CONV), _F32) + cb_ref[...]
        for k in range(CONV_WIDTH):
            acc = acc + cw_ref[k:k + 1, :] * padded(t + k)
        ys.append(acc)
    for j in range(CONV_BUF):
        convo_ref[j] = padded(j + DEC_SEQ)
    yc = _silu(_layernorm(jnp.concatenate(ys, axis=0), clg_ref[...], clb_ref[...]))
    y_a = _dot(yc.astype(_BF), wco_ref[...])

    p = _dot(h, win_ref[:, 1024:1536])
    p_t = [p[t * nb:(t + 1) * nb, :] for t in range(DEC_SEQ)]
    ppad = lambda j: sp_ref[j] if j < POOL_BUF else p_t[j - POOL_BUF]
    ms = []
    for t in range(DEC_SEQ):
        parts = []
        for g, w in enumerate(POOL_WINDOWS):
            sl = slice(g * LANES, (g + 1) * LANES)
            tot = p_t[t][:, sl]
            for j in range(1, w):
                tot = tot + ppad(POOL_BUF + t - j)[:, sl]
            cnt = float(min(PAST_LEN + t + 1, w))
            parts.append(tot / cnt - p_t[t][:, sl])
        ms.append(jnp.concatenate(parts, axis=1))
    for j in range(POOL_BUF):
        poolo_ref[j] = ppad(j + DEC_SEQ)
    y_b = _pool_linear(jnp.concatenate(ms, axis=0), pw_ref, ps_ref, wpo_ref)

    u = _dot(h, win_ref[:, 1536:2048])
    v = _dot(h, win_ref[:, 2048:2560])
    vn = _layernorm(v, slg_ref[...], slb_ref[...])
    vn_ref[...] = vn
    vn_t = [vn[t * nb:(t + 1) * nb, :] for t in range(DEC_SEQ)]
    mixed = []
    for t in range(DEC_SEQ):
        acc = jnp.zeros((nb, D_SGU), _F32) + sb4_ref[t:t + 1, :]
        for src in range(t + 1):
            acc = acc + sw4_ref[t, src:src + 1, :] * vn_t[src]
        mixed.append(acc)
    y_c = _dot((u * jnp.concatenate(mixed, axis=0)).astype(_BF), wso_ref[...])

    xo_ref[...] = _merge_out(x, h, win_ref, wo_ref, y_a, y_b, y_c)


def _mix_sample(x, sc_t, sp_t, layer, prm):
    n_tok = DEC_BATCH * DEC_SEQ
    ls = lambda *tail: _layer_spec(tail, layer, 1)
    in_specs = [
        pl.BlockSpec((n_tok, D_MODEL), lambda i: (0, 0)),
        ls(CONV_BUF, DEC_BATCH, D_CONV),
        ls(POOL_BUF, DEC_BATCH, D_POOL),
        ls(1, D_MODEL),
        ls(D_MODEL, D_IN),
        ls(CONV_WIDTH, D_CONV),
        ls(1, D_CONV), ls(1, D_CONV), ls(1, D_CONV),
        ls(D_CONV, D_MODEL),
        ls(len(POOL_WINDOWS), LANES, LANES),
        ls(1, D_POOL),
        ls(D_POOL, D_MODEL),
        ls(1, D_SGU), ls(1, D_SGU),
        ls(DEC_SEQ, DEC_SEQ, D_SGU),
        ls(DEC_SEQ, D_SGU),
        ls(D_SGU, D_MODEL),
        ls(D_MODEL, D_MODEL),
    ]
    out_specs = [
        pl.BlockSpec((n_tok, D_MODEL), lambda i: (0, 0)),
        pl.BlockSpec((CONV_BUF, DEC_BATCH, D_CONV), lambda i: (0, 0, 0)),
        pl.BlockSpec((POOL_BUF, DEC_BATCH, D_POOL), lambda i: (0, 0, 0)),
        pl.BlockSpec((n_tok, D_SGU), lambda i: (0, 0)),
    ]
    out_shape = [
        jax.ShapeDtypeStruct((n_tok, D_MODEL), _F32),
        jax.ShapeDtypeStruct((CONV_BUF, DEC_BATCH, D_CONV), _F32),
        jax.ShapeDtypeStruct((POOL_BUF, DEC_BATCH, D_POOL), _F32),
        jax.ShapeDtypeStruct((n_tok, D_SGU), _F32),
    ]
    return pl.pallas_call(
        _mix_sample_body,
        grid=(1,),
        in_specs=in_specs, out_specs=out_specs, out_shape=out_shape,
        compiler_params=pltpu.CompilerParams(
            dimension_semantics=("arbitrary",), vmem_limit_bytes=VMEM_LIMIT),
        name="mix_sample",
    )(x, sc_t, sp_t, *prm)


def kernel(x_prompt, x_sample, state_conv, state_pool, ffn1_norm, ffn1_w_gate_up, ffn1_w_down, mix_norm, w_in, conv_dw_w, conv_dw_b, conv_ln_g, conv_ln_b, w_conv_out, pool_w, pool_scale, w_pool_out, sgu_ln_g, sgu_ln_b, sgu_w, sgu_b, w_sgu_out, w_o, ffn2_norm, ffn2_w_gate_up, ffn2_w_down, final_norm):
    row = lambda p: p.reshape(DEPTH, 1, p.shape[-1])
    ffn1 = (row(ffn1_norm), ffn1_w_gate_up.astype(_BF), ffn1_w_down.astype(_BF))
    ffn2 = (row(ffn2_norm), ffn2_w_gate_up.astype(_BF), ffn2_w_down.astype(_BF))
    fnorm = final_norm.reshape(1, D_MODEL)

    shared_head = (row(mix_norm), w_in.astype(_BF))
    conv_vec = (row(conv_dw_b), row(conv_ln_g), row(conv_ln_b), w_conv_out.astype(_BF))
    shared_tail = (pool_w.astype(_BF), row(pool_scale), w_pool_out.astype(_BF),
                   row(sgu_ln_g), row(sgu_ln_b))
    outs = (w_sgu_out.astype(_BF), w_o.astype(_BF))

    cw_slab = conv_dw_w.reshape(DEPTH, CONV_WIDTH, N_SLAB, LANES).transpose(0, 2, 1, 3)
    sb_full = jnp.repeat(jnp.swapaxes(sgu_b, 1, 2), LANES, axis=2)
    prm_prompt = shared_head + (cw_slab,) + conv_vec + shared_tail + (sgu_w, sb_full) + outs

    sw4 = jnp.tril(sgu_w[:, :, :DEC_SEQ, :DEC_SEQ])
    sw4 = jnp.repeat(jnp.transpose(sw4, (0, 2, 3, 1)), LANES, axis=3)
    sb4 = jnp.repeat(jnp.swapaxes(sgu_b[:, :, :DEC_SEQ], 1, 2), LANES, axis=2)
    prm_sample = shared_head + (conv_dw_w,) + conv_vec + shared_tail + (sw4, sb4) + outs

    xp = x_prompt.reshape(BATCH * SEQ, D_MODEL)
    xs = jnp.swapaxes(x_sample, 0, 1).reshape(DEC_SEQ * DEC_BATCH, D_MODEL)
    sc_t = jnp.swapaxes(state_conv, 1, 2)
    sp_t = jnp.swapaxes(state_pool, 1, 2)

    conv_p, pool_p, conv_s, pool_s, vn_s = [], [], [], [], []
    for layer in range(DEPTH):
        last = layer == DEPTH - 1
        xp = _ffn(xp, *ffn1, fnorm, layer, False)
        xs = _ffn(xs, *ffn1, fnorm, layer, False)
        xp, cp, pp = _mix_prompt(xp, layer, prm_prompt)
        xs, cs, ps, vn = _mix_sample(xs, sc_t, sp_t, layer, prm_sample)
        xp = _ffn(xp, *ffn2, fnorm, layer, last)
        xs = _ffn(xs, *ffn2, fnorm, layer, last)
        conv_p.append(cp)
        pool_p.append(pp)
        conv_s.append(jnp.swapaxes(cs, 0, 1))
        pool_s.append(jnp.swapaxes(ps, 0, 1))
        vn_s.append(jnp.swapaxes(vn.reshape(DEC_SEQ, DEC_BATCH, D_SGU), 0, 1))

    y_prompt = xp.reshape(BATCH, SEQ, D_MODEL)
    y_sample = jnp.swapaxes(xs.reshape(DEC_SEQ, DEC_BATCH, D_MODEL), 0, 1)
    return (y_prompt, y_sample, jnp.stack(conv_p), jnp.stack(conv_s),
            jnp.stack(pool_p), jnp.stack(pool_s), jnp.stack(vn_s))
```

```python
import functools

import jax
import jax.numpy as jnp
from jax import lax
from jax.experimental import pallas as pl
from jax.experimental.pallas import tpu as pltpu

D_MODEL = 1024
BATCH = 8
SEQ = 2048
DEPTH = 4
DEC_BATCH = 128
DEC_SEQ = 4
PAST_LEN = 16384
D_CONV = 512
CONV_WIDTH = 31
CONV_BUF = CONV_WIDTH - 1
D_POOL = 512
POOL_WINDOWS = (2, 4, 8, 16)
POOL_BUF = 15
D_SGU = 512
SGU_GROUPS = 4
CHUNK = 128
D_FF = 2816
D_IN = 5632
COL_P = 2 * D_CONV
COL_U = COL_P + D_POOL
COL_V = COL_U + D_SGU
COL_G = COL_V + D_SGU
RMS_EPS = 1e-6
LN_EPS = 1e-5

LANES = 128
N_SLAB = D_CONV // LANES
TM = 512
TM_FFN = 1024
CONV_HALO = 32
POOL_HALO = 16
ROW_BLK = 64
HALF_BLK = ROW_BLK // 2
FF_CHUNKS = ((0, 1024), (1024, 1024), (2048, 768))
VMEM_LIMIT = 56 * 1024 * 1024

_BF = jnp.bfloat16
_F32 = jnp.float32


def _dot(a, b):
    return jnp.dot(a, b, preferred_element_type=_F32)


def _rmsnorm(x, g):
    return x * lax.rsqrt(jnp.mean(x * x, axis=-1, keepdims=True) + RMS_EPS) * g


def _layernorm(x, g, b):
    mu = jnp.mean(x, axis=-1, keepdims=True)
    xc = x - mu
    var = jnp.mean(xc * xc, axis=-1, keepdims=True)
    return xc * lax.rsqrt(var + LN_EPS) * g + b


def _silu(x):
    return x * jax.nn.sigmoid(x)


def _ffn_body(x_ref, g_ref, wgu_ref, wd_ref, fn_ref, o_ref, *, final):
    x = x_ref[...]
    h = _rmsnorm(x, g_ref[...]).astype(_BF)
    acc = None
    for c0, cw in FF_CHUNKS:
        gate = _dot(h, wgu_ref[:, c0:c0 + cw])
        up = _dot(h, wgu_ref[:, D_FF + c0:D_FF + c0 + cw])
        part = _dot((_silu(gate) * up).astype(_BF), wd_ref[c0:c0 + cw, :])
        acc = part if acc is None else acc + part
    y = x + 0.5 * acc
    if final:
        y = _rmsnorm(y, fn_ref[...])
    o_ref[...] = y


def _layer_spec(tail, layer):
    zeros = (0,) * len(tail)
    return pl.BlockSpec((None,) + tuple(tail), lambda *_: (layer,) + zeros,
                        pipeline_mode=pl.Buffered(1))


def _ffn(x, norm, wgu, wd, final_norm, layer, final):
    n_tok = x.shape[0]
    tm = min(TM_FFN, n_tok)
    return pl.pallas_call(
        functools.partial(_ffn_body, final=final),
        grid=(n_tok // tm,),
        in_specs=[
            pl.BlockSpec((tm, D_MODEL), lambda i: (i, 0)),
            _layer_spec((1, D_MODEL), layer),
            _layer_spec((D_MODEL, 2 * D_FF), layer),
            _layer_spec((D_FF, D_MODEL), layer),
            pl.BlockSpec((1, D_MODEL), lambda i: (0, 0)),
        ],
        out_specs=pl.BlockSpec((tm, D_MODEL), lambda i: (i, 0)),
        out_shape=jax.ShapeDtypeStruct((n_tok, D_MODEL), _F32),
        compiler_params=pltpu.CompilerParams(
            dimension_semantics=("arbitrary",), vmem_limit_bytes=VMEM_LIMIT),
        name="ffn",
    )(x, norm, wgu, wd, final_norm)


def _pool_linear(m, pw_ref, ps_ref, wpo_ref):
    y = jnp.concatenate(
        [_dot(m[:, g * LANES:(g + 1) * LANES].astype(_BF), pw_ref[g]) for g in range(len(POOL_WINDOWS))],
        axis=1)
    return _dot((y * ps_ref[...]).astype(_BF), wpo_ref[...])


def _merge_out(x, h, win_ref, wo_ref, y_a, y_b, y_c):
    gl = jax.nn.sigmoid(_dot(h, win_ref[:, COL_G:D_IN]))
    merged = (gl[:, 0:D_MODEL] * y_a + gl[:, D_MODEL:2 * D_MODEL] * y_b
              + gl[:, 2 * D_MODEL:3 * D_MODEL] * y_c)
    return x + _dot(merged.astype(_BF), wo_ref[...])


def _mix_prompt_body(x_ref, g_ref, win_ref, cw_ref, cb_ref, clg_ref, clb_ref, wco_ref,
                     pw_ref, ps_ref, wpo_ref, slg_ref, slb_ref, sw_ref, sb_ref, wso_ref, wo_ref,
                     xo_ref, convo_ref, poolo_ref,
                     ah_ref, ph_ref, yc_ref, m_ref):
    s = pl.program_id(1)

    @pl.when(s == 0)
    def _():
        for g in range(N_SLAB):
            ah_ref[g, 0:CONV_HALO, :] = jnp.zeros((CONV_HALO, LANES), _F32)
            ph_ref[g, 0:POOL_HALO, :] = jnp.zeros((POOL_HALO, LANES), _F32)

    x = x_ref[...]
    h = _rmsnorm(x, g_ref[...]).astype(_BF)

    a2 = _dot(h, win_ref[:, 0:COL_P])
    a = a2[:, 0:D_CONV] * jax.nn.sigmoid(a2[:, D_CONV:2 * D_CONV])
    for g in range(N_SLAB):
        ah_ref[g, CONV_HALO:CONV_HALO + TM, :] = a[:, g * LANES:(g + 1) * LANES]

    for i in range(N_SLAB * (TM // ROW_BLK)):
        g = i // (TM // ROW_BLK)
        base = (i % (TM // ROW_BLK)) * ROW_BLK
        for ph in range(2):
            acc = jnp.zeros((HALF_BLK, LANES), _F32)
            for k in range(CONV_WIDTH):
                start = base + (CONV_HALO - CONV_BUF) + k + ph
                acc = acc + cw_ref[g, k:k + 1, :] * ah_ref[g, pl.ds(start, HALF_BLK, stride=2), :]
            yc_ref[g, pl.ds(base + ph, HALF_BLK, stride=2), :] = acc

    yc = jnp.concatenate([yc_ref[g] for g in range(N_SLAB)], axis=1) + cb_ref[...]
    yc = _silu(_layernorm(yc, clg_ref[...], clb_ref[...]))
    y_a = _dot(yc.astype(_BF), wco_ref[...])

    p = _dot(h, win_ref[:, COL_P:COL_U])
    for g in range(N_SLAB):
        ph_ref[g, POOL_HALO:POOL_HALO + TM, :] = p[:, g * LANES:(g + 1) * LANES]

    for i in range(TM // ROW_BLK):
        base = i * ROW_BLK
        for ph in range(2):
            pos = s * TM + base + ph + 2 * lax.broadcasted_iota(jnp.int32, (HALF_BLK, LANES), 0)
            for g, w in enumerate(POOL_WINDOWS):
                cur = ph_ref[g, pl.ds(base + POOL_HALO + ph, HALF_BLK, stride=2), :]
                tot = cur
                for j in range(1, w):
                    tot = tot + ph_ref[g, pl.ds(base + POOL_HALO + ph - j, HALF_BLK, stride=2), :]
                cnt = jnp.minimum(pos + 1, w).astype(_F32)
                m_ref[g, pl.ds(base + ph, HALF_BLK, stride=2), :] = tot / cnt - cur

    m = jnp.concatenate([m_ref[g] for g in range(N_SLAB)], axis=1)
    y_b = _pool_linear(m, pw_ref, ps_ref, wpo_ref)

    u = _dot(h, win_ref[:, COL_U:COL_V])
    v = _dot(h, win_ref[:, COL_V:COL_G])
    vn = _layernorm(v, slg_ref[...], slb_ref[...]).astype(_BF)
    n_chunk = TM // CHUNK
    row = lax.broadcasted_iota(jnp.int32, (CHUNK, CHUNK), 0)
    col = lax.broadcasted_iota(jnp.int32, (CHUNK, CHUNK), 1)
    mixed_groups = []
    for g in range(SGU_GROUPS):
        w_causal = jnp.where(row >= col, sw_ref[g], 0.0).astype(_BF)
        vg = vn[:, g * LANES:(g + 1) * LANES]
        rhs = jnp.concatenate([vg[c * CHUNK:(c + 1) * CHUNK, :] for c in range(n_chunk)], axis=1)
        res = _dot(w_causal, rhs)
        mixed_groups.append(
            jnp.concatenate([res[:, c * LANES:(c + 1) * LANES] for c in range(n_chunk)], axis=0))
    bias = jnp.concatenate([sb_ref[...]] * n_chunk, axis=0)
    mixed = jnp.concatenate(mixed_groups, axis=1) + bias
    y_c = _dot((u * mixed).astype(_BF), wso_ref[...])

    xo_ref[...] = _merge_out(x, h, win_ref, wo_ref, y_a, y_b, y_c)

    for g in range(N_SLAB):
        convo_ref[:, g * LANES:(g + 1) * LANES] = ah_ref[g, CONV_HALO + TM - CONV_BUF:CONV_HALO + TM, :]
        poolo_ref[:, g * LANES:(g + 1) * LANES] = ph_ref[g, POOL_HALO + TM - POOL_BUF:POOL_HALO + TM, :]
        ah_ref[g, 0:CONV_HALO, :] = ah_ref[g, TM:TM + CONV_HALO, :]
        ph_ref[g, 0:POOL_HALO, :] = ph_ref[g, TM:TM + POOL_HALO, :]


def _mix_prompt(x, layer, prm):
    n_s = SEQ // TM
    tok = pl.BlockSpec((TM, D_MODEL), lambda b, s: (b * n_s + s, 0))
    ls = lambda *tail: _layer_spec(tail, layer)
    in_specs = [
        tok,
        ls(1, D_MODEL),
        ls(D_MODEL, D_IN),
        ls(N_SLAB, CONV_WIDTH, LANES),
        ls(1, D_CONV), ls(1, D_CONV), ls(1, D_CONV),
        ls(D_CONV, D_MODEL),
        ls(len(POOL_WINDOWS), LANES, LANES),
        ls(1, D_POOL),
        ls(D_POOL, D_MODEL),
        ls(1, D_SGU), ls(1, D_SGU),
        ls(SGU_GROUPS, CHUNK, CHUNK),
        ls(CHUNK, D_SGU),
        ls(D_SGU, D_MODEL),
        ls(D_MODEL, D_MODEL),
    ]
    out_specs = [
        tok,
        pl.BlockSpec((None, CONV_BUF, D_CONV), lambda b, s: (b, 0, 0)),
        pl.BlockSpec((None, POOL_BUF, D_POOL), lambda b, s: (b, 0, 0)),
    ]
    out_shape = [
        jax.ShapeDtypeStruct((BATCH * SEQ, D_MODEL), _F32),
        jax.ShapeDtypeStruct((BATCH, CONV_BUF, D_CONV), _F32),
        jax.ShapeDtypeStruct((BATCH, POOL_BUF, D_POOL), _F32),
    ]
    scratch = [
        pltpu.VMEM((N_SLAB, CONV_HALO + TM, LANES), _F32),
        pltpu.VMEM((N_SLAB, POOL_HALO + TM, LANES), _F32),
        pltpu.VMEM((N_SLAB, TM, LANES), _F32),
        pltpu.VMEM((N_SLAB, TM, LANES), _F32),
    ]
    return pl.pallas_call(
        _mix_prompt_body,
        grid=(BATCH, n_s),
        in_specs=in_specs, out_specs=out_specs, out_shape=out_shape,
        scratch_shapes=scratch,
        compiler_params=pltpu.CompilerParams(
            dimension_semantics=("arbitrary", "arbitrary"), vmem_limit_bytes=VMEM_LIMIT),
        name="mix_prompt",
    )(x, *prm)


def _mix_sample_body(x_ref, sc_ref, sp_ref, g_ref, win_ref, cw_ref, cb_ref, clg_ref, clb_ref, wco_ref,
                     pw_ref, ps_ref, wpo_ref, slg_ref, slb_ref, sw4_ref, sb4_ref, wso_ref, wo_ref,
                     xo_ref, convo_ref, poolo_ref, vn_ref):
    nb = DEC_BATCH
    x = x_ref[...]
    h = _rmsnorm(x, g_ref[...]).astype(_BF)

    a2 = _dot(h, win_ref[:, 0:COL_P])
    a = a2[:, 0:D_CONV] * jax.nn.sigmoid(a2[:, D_CONV:2 * D_CONV])
    a_t = [a[t * nb:(t + 1) * nb, :] for t in range(DEC_SEQ)]
    padded = lambda j: sc_ref[j] if j < CONV_BUF else a_t[j - CONV_BUF]
    ys = []
    for t in range(DEC_SEQ):
        acc = jnp.zeros((nb, D_CONV), _F32) + cb_ref[...]
        for k in range(CONV_WIDTH):
            acc = acc + cw_ref[k:k + 1, :] * padded(t + k)
        ys.append(acc)
    for j in range(CONV_BUF):
        convo_ref[j] = padded(j + DEC_SEQ)
    yc = _silu(_layernorm(jnp.concatenate(ys, axis=0), clg_ref[...], clb_ref[...]))
    y_a = _dot(yc.astype(_BF), wco_ref[...])

    p = _dot(h, win_ref[:, COL_P:COL_U])
    p_t = [p[t * nb:(t + 1) * nb, :] for t in range(DEC_SEQ)]
    ppad = lambda j: sp_ref[j] if j < POOL_BUF else p_t[j - POOL_BUF]
    ms = []
    for t in range(DEC_SEQ):
        parts = []
        for g, w in enumerate(POOL_WINDOWS):
            sl = slice(g * LANES, (g + 1) * LANES)
            tot = p_t[t][:, sl]
            for j in range(1, w):
                tot = tot + ppad(POOL_BUF + t - j)[:, sl]
            cnt = float(min(PAST_LEN + t + 1, w))
            parts.append(tot / cnt - p_t[t][:, sl])
        ms.append(jnp.concatenate(parts, axis=1))
    for j in range(POOL_BUF):
        poolo_ref[j] = ppad(j + DEC_SEQ)
    y_b = _pool_linear(jnp.concatenate(ms, axis=0), pw_ref, ps_ref, wpo_ref)

    u = _dot(h, win_ref[:, COL_U:COL_V])
    v = _dot(h, win_ref[:, COL_V:COL_G])
    vn = _layernorm(v, slg_ref[...], slb_ref[...])
    vn_ref[...] = vn
    vn_t = [vn[t * nb:(t + 1) * nb, :] for t in range(DEC_SEQ)]
    mixed = []
    for t in range(DEC_SEQ):
        acc = jnp.zeros((nb, D_SGU), _F32) + sb4_ref[t:t + 1, :]
        for src in range(t + 1):
            acc = acc + sw4_ref[t, src:src + 1, :] * vn_t[src]
        mixed.append(acc)
    y_c = _dot((u * jnp.concatenate(mixed, axis=0)).astype(_BF), wso_ref[...])

    xo_ref[...] = _merge_out(x, h, win_ref, wo_ref, y_a, y_b, y_c)


def _mix_sample(x, sc_t, sp_t, layer, prm):
    n_tok = DEC_BATCH * DEC_SEQ
    ls = lambda *tail: _layer_spec(tail, layer)
    in_specs = [
        pl.BlockSpec((n_tok, D_MODEL), lambda i: (0, 0)),
        ls(CONV_BUF, DEC_BATCH, D_CONV),
        ls(POOL_BUF, DEC_BATCH, D_POOL),
        ls(1, D_MODEL),
        ls(D_MODEL, D_IN),
        ls(CONV_WIDTH, D_CONV),
        ls(1, D_CONV), ls(1, D_CONV), ls(1, D_CONV),
        ls(D_CONV, D_MODEL),
        ls(len(POOL_WINDOWS), LANES, LANES),
        ls(1, D_POOL),
        ls(D_POOL, D_MODEL),
        ls(1, D_SGU), ls(1, D_SGU),
        ls(DEC_SEQ, DEC_SEQ, D_SGU),
        ls(DEC_SEQ, D_SGU),
        ls(D_SGU, D_MODEL),
        ls(D_MODEL, D_MODEL),
    ]
    out_specs = [
        pl.BlockSpec((n_tok, D_MODEL), lambda i: (0, 0)),
        pl.BlockSpec((CONV_BUF, DEC_BATCH, D_CONV), lambda i: (0, 0, 0)),
        pl.BlockSpec((POOL_BUF, DEC_BATCH, D_POOL), lambda i: (0, 0, 0)),
        pl.BlockSpec((n_tok, D_SGU), lambda i: (0, 0)),
    ]
    out_shape = [
        jax.ShapeDtypeStruct((n_tok, D_MODEL), _F32),
        jax.ShapeDtypeStruct((CONV_BUF, DEC_BATCH, D_CONV), _F32),
        jax.ShapeDtypeStruct((POOL_BUF, DEC_BATCH, D_POOL), _F32),
        jax.ShapeDtypeStruct((n_tok, D_SGU), _F32),
    ]
    return pl.pallas_call(
        _mix_sample_body,
        grid=(1,),
        in_specs=in_specs, out_specs=out_specs, out_shape=out_shape,
        compiler_params=pltpu.CompilerParams(
            dimension_semantics=("arbitrary",), vmem_limit_bytes=VMEM_LIMIT),
        name="mix_sample",
    )(x, sc_t, sp_t, *prm)


def kernel(x_prompt, x_sample, state_conv, state_pool, ffn1_norm, ffn1_w_gate_up, ffn1_w_down, mix_norm, w_in, conv_dw_w, conv_dw_b, conv_ln_g, conv_ln_b, w_conv_out, pool_w, pool_scale, w_pool_out, sgu_ln_g, sgu_ln_b, sgu_w, sgu_b, w_sgu_out, w_o, ffn2_norm, ffn2_w_gate_up, ffn2_w_down, final_norm):
    row = lambda p: p.reshape(DEPTH, 1, p.shape[-1])
    ffn1 = (row(ffn1_norm), ffn1_w_gate_up.astype(_BF), ffn1_w_down.astype(_BF))
    ffn2 = (row(ffn2_norm), ffn2_w_gate_up.astype(_BF), ffn2_w_down.astype(_BF))
    fnorm = final_norm.reshape(1, D_MODEL)

    shared_head = (row(mix_norm), w_in.astype(_BF))
    conv_vec = (row(conv_dw_b), row(conv_ln_g), row(conv_ln_b), w_conv_out.astype(_BF))
    shared_tail = (pool_w.astype(_BF), row(pool_scale), w_pool_out.astype(_BF),
                   row(sgu_ln_g), row(sgu_ln_b))
    outs = (w_sgu_out.astype(_BF), w_o.astype(_BF))

    cw_slab = conv_dw_w.reshape(DEPTH, CONV_WIDTH, N_SLAB, LANES).transpose(0, 2, 1, 3)
    sb_full = jnp.repeat(jnp.swapaxes(sgu_b, 1, 2), LANES, axis=2)
    prm_prompt = shared_head + (cw_slab,) + conv_vec + shared_tail + (sgu_w, sb_full) + outs

    sw4 = jnp.tril(sgu_w[:, :, :DEC_SEQ, :DEC_SEQ])
    sw4 = jnp.repeat(jnp.transpose(sw4, (0, 2, 3, 1)), LANES, axis=3)
    sb4 = jnp.repeat(jnp.swapaxes(sgu_b[:, :, :DEC_SEQ], 1, 2), LANES, axis=2)
    prm_sample = shared_head + (conv_dw_w,) + conv_vec + shared_tail + (sw4, sb4) + outs

    xp = x_prompt.reshape(BATCH * SEQ, D_MODEL)
    xs = jnp.swapaxes(x_sample, 0, 1).reshape(DEC_SEQ * DEC_BATCH, D_MODEL)
    sc_t = jnp.swapaxes(state_conv, 1, 2)
    sp_t = jnp.swapaxes(state_pool, 1, 2)

    conv_p, pool_p, conv_s, pool_s, vn_s = [], [], [], [], []
    for layer in range(DEPTH):
        last = layer == DEPTH - 1
        xp = _ffn(xp, *ffn1, fnorm, layer, False)
        xs = _ffn(xs, *ffn1, fnorm, layer, False)
        xp, cp, pp = _mix_prompt(xp, layer, prm_prompt)
        xs, cs, ps, vn = _mix_sample(xs, sc_t, sp_t, layer, prm_sample)
        xp = _ffn(xp, *ffn2, fnorm, layer, last)
        xs = _ffn(xs, *ffn2, fnorm, layer, last)
        conv_p.append(cp)
        pool_p.append(pp)
        conv_s.append(jnp.swapaxes(cs, 0, 1))
        pool_s.append(jnp.swapaxes(ps, 0, 1))
        vn_s.append(jnp.swapaxes(vn.reshape(DEC_SEQ, DEC_BATCH, D_SGU), 0, 1))

    y_prompt = xp.reshape(BATCH, SEQ, D_MODEL)
    y_sample = jnp.swapaxes(xs.reshape(DEC_SEQ, DEC_BATCH, D_MODEL), 0, 1)
    return (y_prompt, y_sample, jnp.stack(conv_p), jnp.stack(conv_s),
            jnp.stack(pool_p), jnp.stack(pool_s), jnp.stack(vn_s))
```

```python
import functools

import jax
import jax.numpy as jnp
from jax import lax
from jax.experimental import pallas as pl
from jax.experimental.pallas import tpu as pltpu

D_MODEL = 1024
BATCH = 8
SEQ = 2048
DEPTH = 4
DEC_BATCH = 128
DEC_SEQ = 4
PAST_LEN = 16384
D_CONV = 512
CONV_WIDTH = 31
CONV_BUF = CONV_WIDTH - 1
D_POOL = 512
POOL_WINDOWS = (2, 4, 8, 16)
POOL_BUF = 15
D_SGU = 512
SGU_GROUPS = 4
CHUNK = 128
D_FF = 2816
D_IN = 5632
COL_P = 2 * D_CONV
COL_U = COL_P + D_POOL
COL_V = COL_U + D_SGU
COL_G = COL_V + D_SGU
RMS_EPS = 1e-6
LN_EPS = 1e-5

LANES = 128
N_SLAB = D_CONV // LANES
TM = 512
N_PROMPT_TILES = BATCH * SEQ // TM
CONV_HALO = 32
POOL_HALO = 16
ROW_BLK = 64
HALF_BLK = ROW_BLK // 2
FF_CHUNKS = ((0, 1024), (1024, 1024), (2048, 768))
N_CVT = 16
VMEM_LIMIT = 56 * 1024 * 1024

_BF = jnp.bfloat16
_F32 = jnp.float32


def _dot(a, b):
    return jnp.dot(a, b, preferred_element_type=_F32)


def _rmsnorm(x, g):
    return x * lax.rsqrt(jnp.mean(x * x, axis=-1, keepdims=True) + RMS_EPS) * g


def _layernorm(x, g, b):
    mu = jnp.mean(x, axis=-1, keepdims=True)
    xc = x - mu
    var = jnp.mean(xc * xc, axis=-1, keepdims=True)
    return xc * lax.rsqrt(var + LN_EPS) * g + b


def _silu(x):
    return x * jax.nn.sigmoid(x)


def _layer_spec(tail, layer):
    zeros = (0,) * len(tail)
    return pl.BlockSpec((None,) + tuple(tail), lambda *_: (layer,) + zeros,
                        pipeline_mode=pl.Buffered(1))


def _whole_spec(shape):
    zeros = (0,) * len(shape)
    return pl.BlockSpec(tuple(shape), lambda *_: zeros, pipeline_mode=pl.Buffered(1))


def _cvt_specs(stacked, layer, step_of):
    in_specs, out_specs, out_shape = [], [], []
    for w in stacked:
        _, rows, cols = w.shape
        blk = rows // N_CVT
        assert blk * N_CVT == rows and blk % 16 == 0, w.shape
        at = lambda *g: jnp.minimum(step_of(*g), N_CVT - 1)
        in_specs.append(pl.BlockSpec((None, blk, cols), lambda *g, at=at: (layer, at(*g), 0)))
        out_specs.append(pl.BlockSpec((blk, cols), lambda *g, at=at: (at(*g), 0)))
        out_shape.append(jax.ShapeDtypeStruct((rows, cols), _BF))
    return in_specs, out_specs, out_shape


def _round_weights(src_refs, dst_refs):
    for src, dst in zip(src_refs, dst_refs):
        dst[...] = src[...].astype(_BF)


def _ffn_tile(x, g_ref, wgu_ref, wd_ref, fn_ref, final):
    h = _rmsnorm(x, g_ref[...]).astype(_BF)
    acc = None
    for c0, cw in FF_CHUNKS:
        gate = _dot(h, wgu_ref[:, c0:c0 + cw])
        up = _dot(h, wgu_ref[:, D_FF + c0:D_FF + c0 + cw])
        part = _dot((_silu(gate) * up).astype(_BF), wd_ref[c0:c0 + cw, :])
        acc = part if acc is None else acc + part
    y = x + 0.5 * acc
    if final:
        y = _rmsnorm(y, fn_ref[...])
    return y


def _ffn_body(*refs, n_cvt, final):
    xp_ref, xs_ref, g_ref, wgu_ref, wd_ref, fn_ref = refs[:6]
    cvt_in = refs[6:6 + n_cvt]
    op_ref, os_ref = refs[6 + n_cvt:8 + n_cvt]
    cvt_out = refs[8 + n_cvt:]
    i = pl.program_id(0)

    @pl.when(i < N_PROMPT_TILES)
    def _():
        op_ref[...] = _ffn_tile(xp_ref[...], g_ref, wgu_ref, wd_ref, fn_ref, final)
        _round_weights(cvt_in, cvt_out)

    @pl.when(i == N_PROMPT_TILES)
    def _():
        os_ref[...] = _ffn_tile(xs_ref[...], g_ref, wgu_ref, wd_ref, fn_ref, final)


def _ffn(xp, xs, norm, wgu, wd, final_norm, layer, final, to_round, round_layer):
    last_p = N_PROMPT_TILES - 1
    cvt_in, cvt_out, cvt_shape = _cvt_specs(to_round, round_layer, lambda i: i)
    prompt_tile = pl.BlockSpec((TM, D_MODEL), lambda i: (jnp.minimum(i, last_p), 0))
    sample_tile = pl.BlockSpec((TM, D_MODEL), lambda i: (0, 0))
    out = pl.pallas_call(
        functools.partial(_ffn_body, n_cvt=len(to_round), final=final),
        grid=(N_PROMPT_TILES + 1,),
        in_specs=[
            prompt_tile,
            pl.BlockSpec((TM, D_MODEL), lambda i: (0, 0), pipeline_mode=pl.Buffered(1)),
            _layer_spec((1, D_MODEL), layer),
            _whole_spec((D_MODEL, 2 * D_FF)),
            _whole_spec((D_FF, D_MODEL)),
            pl.BlockSpec((1, D_MODEL), lambda i: (0, 0)),
        ] + cvt_in,
        out_specs=[prompt_tile, sample_tile] + cvt_out,
        out_shape=[jax.ShapeDtypeStruct(xp.shape, _F32), jax.ShapeDtypeStruct(xs.shape, _F32)] + cvt_shape,
        compiler_params=pltpu.CompilerParams(
            dimension_semantics=("arbitrary",), vmem_limit_bytes=VMEM_LIMIT),
        name="ffn",
    )(xp, xs, norm, wgu, wd, final_norm, *to_round)
    return out[0], out[1], tuple(out[2:])


def _pool_linear(m, pw_ref, ps_ref, wpo_ref):
    y = jnp.concatenate(
        [_dot(m[:, g * LANES:(g + 1) * LANES].astype(_BF), pw_ref[g * LANES:(g + 1) * LANES, :])
         for g in range(len(POOL_WINDOWS))], axis=1)
    return _dot((y * ps_ref[...]).astype(_BF), wpo_ref[...])


def _merge_out(x, h, win_ref, wo_ref, y_a, y_b, y_c):
    gl = jax.nn.sigmoid(_dot(h, win_ref[:, COL_G:D_IN]))
    merged = (gl[:, 0:D_MODEL] * y_a + gl[:, D_MODEL:2 * D_MODEL] * y_b
              + gl[:, 2 * D_MODEL:3 * D_MODEL] * y_c)
    return x + _dot(merged.astype(_BF), wo_ref[...])


def _mix_prompt_body(*refs, n_cvt):
    (x_ref, g_ref, win_ref, cw_ref, cb_ref, clg_ref, clb_ref, wco_ref,
     pw_ref, ps_ref, wpo_ref, slg_ref, slb_ref, sw_ref, sb_ref, wso_ref, wo_ref) = refs[:17]
    cvt_in = refs[17:17 + n_cvt]
    xo_ref, convo_ref, poolo_ref = refs[17 + n_cvt:20 + n_cvt]
    cvt_out = refs[20 + n_cvt:20 + 2 * n_cvt]
    ah_ref, ph_ref, yc_ref, m_ref = refs[20 + 2 * n_cvt:]
    s = pl.program_id(1)

    @pl.when(s == 0)
    def _():
        for g in range(N_SLAB):
            ah_ref[g, 0:CONV_HALO, :] = jnp.zeros((CONV_HALO, LANES), _F32)
            ph_ref[g, 0:POOL_HALO, :] = jnp.zeros((POOL_HALO, LANES), _F32)

    x = x_ref[...]
    h = _rmsnorm(x, g_ref[...]).astype(_BF)

    a2 = _dot(h, win_ref[:, 0:COL_P])
    a = a2[:, 0:D_CONV] * jax.nn.sigmoid(a2[:, D_CONV:2 * D_CONV])
    for g in range(N_SLAB):
        ah_ref[g, CONV_HALO:CONV_HALO + TM, :] = a[:, g * LANES:(g + 1) * LANES]

    for i in range(N_SLAB * (TM // ROW_BLK)):
        g = i // (TM // ROW_BLK)
        base = (i % (TM // ROW_BLK)) * ROW_BLK
        for ph in range(2):
            acc = jnp.zeros((HALF_BLK, LANES), _F32)
            for k in range(CONV_WIDTH):
                start = base + (CONV_HALO - CONV_BUF) + k + ph
                acc = acc + cw_ref[g, k:k + 1, :] * ah_ref[g, pl.ds(start, HALF_BLK, stride=2), :]
            yc_ref[g, pl.ds(base + ph, HALF_BLK, stride=2), :] = acc

    yc = jnp.concatenate([yc_ref[g] for g in range(N_SLAB)], axis=1) + cb_ref[...]
    yc = _silu(_layernorm(yc, clg_ref[...], clb_ref[...]))
    y_a = _dot(yc.astype(_BF), wco_ref[...])

    p = _dot(h, win_ref[:, COL_P:COL_U])
    for g in range(N_SLAB):
        ph_ref[g, POOL_HALO:POOL_HALO + TM, :] = p[:, g * LANES:(g + 1) * LANES]

    for i in range(TM // ROW_BLK):
        base = i * ROW_BLK
        for ph in range(2):
            pos = s * TM + base + ph + 2 * lax.broadcasted_iota(jnp.int32, (HALF_BLK, LANES), 0)
            for g, w in enumerate(POOL_WINDOWS):
                cur = ph_ref[g, pl.ds(base + POOL_HALO + ph, HALF_BLK, stride=2), :]
                tot = cur
                for j in range(1, w):
                    tot = tot + ph_ref[g, pl.ds(base + POOL_HALO + ph - j, HALF_BLK, stride=2), :]
                cnt = jnp.minimum(pos + 1, w).astype(_F32)
                m_ref[g, pl.ds(base + ph, HALF_BLK, stride=2), :] = tot / cnt - cur

    m = jnp.concatenate([m_ref[g] for g in range(N_SLAB)], axis=1)
    y_b = _pool_linear(m, pw_ref, ps_ref, wpo_ref)

    u = _dot(h, win_ref[:, COL_U:COL_V])
    v = _dot(h, win_ref[:, COL_V:COL_G])
    vn = _layernorm(v, slg_ref[...], slb_ref[...]).astype(_BF)
    n_chunk = TM // CHUNK
    row = lax.broadcasted_iota(jnp.int32, (CHUNK, CHUNK), 0)
    col = lax.broadcasted_iota(jnp.int32, (CHUNK, CHUNK), 1)
    mixed_groups = []
    for g in range(SGU_GROUPS):
        w_causal = jnp.where(row >= col, sw_ref[g], 0.0).astype(_BF)
        vg = vn[:, g * LANES:(g + 1) * LANES]
        rhs = jnp.concatenate([vg[c * CHUNK:(c + 1) * CHUNK, :] for c in range(n_chunk)], axis=1)
        res = _dot(w_causal, rhs)
        mixed_groups.append(
            jnp.concatenate([res[:, c * LANES:(c + 1) * LANES] for c in range(n_chunk)], axis=0))
    bias = jnp.concatenate([sb_ref[...]] * n_chunk, axis=0)
    mixed = jnp.concatenate(mixed_groups, axis=1) + bias
    y_c = _dot((u * mixed).astype(_BF), wso_ref[...])

    xo_ref[...] = _merge_out(x, h, win_ref, wo_ref, y_a, y_b, y_c)

    for g in range(N_SLAB):
        convo_ref[:, g * LANES:(g + 1) * LANES] = ah_ref[g, CONV_HALO + TM - CONV_BUF:CONV_HALO + TM, :]
        poolo_ref[:, g * LANES:(g + 1) * LANES] = ph_ref[g, POOL_HALO + TM - POOL_BUF:POOL_HALO + TM, :]
        ah_ref[g, 0:CONV_HALO, :] = ah_ref[g, TM:TM + CONV_HALO, :]
        ph_ref[g, 0:POOL_HALO, :] = ph_ref[g, TM:TM + POOL_HALO, :]

    _round_weights(cvt_in, cvt_out)


def _mixer_weight_specs(layer, sgu_w_tail, sgu_b_tail, conv_w_tail):
    ls = lambda *tail: _layer_spec(tail, layer)
    return [
        ls(1, D_MODEL),
        _whole_spec((D_MODEL, D_IN)),
        ls(*conv_w_tail),
        ls(1, D_CONV), ls(1, D_CONV), ls(1, D_CONV),
        _whole_spec((D_CONV, D_MODEL)),
        _whole_spec((D_POOL, LANES)),
        ls(1, D_POOL),
        _whole_spec((D_POOL, D_MODEL)),
        ls(1, D_SGU), ls(1, D_SGU),
        ls(*sgu_w_tail),
        ls(*sgu_b_tail),
        _whole_spec((D_SGU, D_MODEL)),
        _whole_spec((D_MODEL, D_MODEL)),
    ]


def _mix_prompt(x, layer, prm, to_round):
    n_s = SEQ // TM
    tok = pl.BlockSpec((TM, D_MODEL), lambda b, s: (b * n_s + s, 0))
    cvt_in, cvt_out, cvt_shape = _cvt_specs(to_round, layer, lambda b, s: b * n_s + s)
    in_specs = [tok] + _mixer_weight_specs(
        layer, (SGU_GROUPS, CHUNK, CHUNK), (CHUNK, D_SGU), (N_SLAB, CONV_WIDTH, LANES)) + cvt_in
    out_specs = [
        tok,
        pl.BlockSpec((None, CONV_BUF, D_CONV), lambda b, s: (b, 0, 0)),
        pl.BlockSpec((None, POOL_BUF, D_POOL), lambda b, s: (b, 0, 0)),
    ] + cvt_out
    out_shape = [
        jax.ShapeDtypeStruct((BATCH * SEQ, D_MODEL), _F32),
        jax.ShapeDtypeStruct((BATCH, CONV_BUF, D_CONV), _F32),
        jax.ShapeDtypeStruct((BATCH, POOL_BUF, D_POOL), _F32),
    ] + cvt_shape
    scratch = [
        pltpu.VMEM((N_SLAB, CONV_HALO + TM, LANES), _F32),
        pltpu.VMEM((N_SLAB, POOL_HALO + TM, LANES), _F32),
        pltpu.VMEM((N_SLAB, TM, LANES), _F32),
        pltpu.VMEM((N_SLAB, TM, LANES), _F32),
    ]
    out = pl.pallas_call(
        functools.partial(_mix_prompt_body, n_cvt=len(to_round)),
        grid=(BATCH, n_s),
        in_specs=in_specs, out_specs=out_specs, out_shape=out_shape,
        scratch_shapes=scratch,
        compiler_params=pltpu.CompilerParams(
            dimension_semantics=("arbitrary", "arbitrary"), vmem_limit_bytes=VMEM_LIMIT),
        name="mix_prompt",
    )(x, *prm, *to_round)
    return out[0], out[1], out[2], tuple(out[3:])


def _mix_sample_body(x_ref, sc_ref, sp_ref, g_ref, win_ref, cw_ref, cb_ref, clg_ref, clb_ref, wco_ref,
                     pw_ref, ps_ref, wpo_ref, slg_ref, slb_ref, sw4_ref, sb4_ref, wso_ref, wo_ref,
                     xo_ref, convo_ref, poolo_ref, vn_ref):
    nb = DEC_BATCH
    x = x_ref[...]
    h = _rmsnorm(x, g_ref[...]).astype(_BF)

    a2 = _dot(h, win_ref[:, 0:COL_P])
    a = a2[:, 0:D_CONV] * jax.nn.sigmoid(a2[:, D_CONV:2 * D_CONV])
    a_t = [a[t * nb:(t + 1) * nb, :] for t in range(DEC_SEQ)]
    padded = lambda j: sc_ref[j] if j < CONV_BUF else a_t[j - CONV_BUF]
    ys = []
    for t in range(DEC_SEQ):
        acc = jnp.zeros((nb, D_CONV), _F32) + cb_ref[...]
        for k in range(CONV_WIDTH):
            acc = acc + cw_ref[k:k + 1, :] * padded(t + k)
        ys.append(acc)
    for j in range(CONV_BUF):
        convo_ref[j] = padded(j + DEC_SEQ)
    yc = _silu(_layernorm(jnp.concatenate(ys, axis=0), clg_ref[...], clb_ref[...]))
    y_a = _dot(yc.astype(_BF), wco_ref[...])

    p = _dot(h, win_ref[:, COL_P:COL_U])
    p_t = [p[t * nb:(t + 1) * nb, :] for t in range(DEC_SEQ)]
    ppad = lambda j: sp_ref[j] if j < POOL_BUF else p_t[j - POOL_BUF]
    ms = []
    for t in range(DEC_SEQ):
        parts = []
        for g, w in enumerate(POOL_WINDOWS):
            sl = slice(g * LANES, (g + 1) * LANES)
            tot = p_t[t][:, sl]
            for j in range(1, w):
                tot = tot + ppad(POOL_BUF + t - j)[:, sl]
            cnt = float(min(PAST_LEN + t + 1, w))
            parts.append(tot / cnt - p_t[t][:, sl])
        ms.append(jnp.concatenate(parts, axis=1))
    for j in range(POOL_BUF):
        poolo_ref[j] = ppad(j + DEC_SEQ)
    y_b = _pool_linear(jnp.concatenate(ms, axis=0), pw_ref, ps_ref, wpo_ref)

    u = _dot(h, win_ref[:, COL_U:COL_V])
    v = _dot(h, win_ref[:, COL_V:COL_G])
    vn = _layernorm(v, slg_ref[...], slb_ref[...])
    vn_ref[...] = vn
    vn_t = [vn[t * nb:(t + 1) * nb, :] for t in range(DEC_SEQ)]
    mixed = []
    for t in range(DEC_SEQ):
        acc = jnp.zeros((nb, D_SGU), _F32) + sb4_ref[t:t + 1, :]
        for src in range(t + 1):
            acc = acc + sw4_ref[t, src:src + 1, :] * vn_t[src]
        mixed.append(acc)
    y_c = _dot((u * jnp.concatenate(mixed, axis=0)).astype(_BF), wso_ref[...])

    xo_ref[...] = _merge_out(x, h, win_ref, wo_ref, y_a, y_b, y_c)


def _mix_sample(x, sc_t, sp_t, layer, prm):
    n_tok = DEC_BATCH * DEC_SEQ
    in_specs = [
        pl.BlockSpec((n_tok, D_MODEL), lambda i: (0, 0)),
        _layer_spec((CONV_BUF, DEC_BATCH, D_CONV), layer),
        _layer_spec((POOL_BUF, DEC_BATCH, D_POOL), layer),
    ] + _mixer_weight_specs(layer, (DEC_SEQ, DEC_SEQ, D_SGU), (DEC_SEQ, D_SGU), (CONV_WIDTH, D_CONV))
    out_specs = [
        pl.BlockSpec((n_tok, D_MODEL), lambda i: (0, 0)),
        pl.BlockSpec((CONV_BUF, DEC_BATCH, D_CONV), lambda i: (0, 0, 0)),
        pl.BlockSpec((POOL_BUF, DEC_BATCH, D_POOL), lambda i: (0, 0, 0)),
        pl.BlockSpec((n_tok, D_SGU), lambda i: (0, 0)),
    ]
    out_shape = [
        jax.ShapeDtypeStruct((n_tok, D_MODEL), _F32),
        jax.ShapeDtypeStruct((CONV_BUF, DEC_BATCH, D_CONV), _F32),
        jax.ShapeDtypeStruct((POOL_BUF, DEC_BATCH, D_POOL), _F32),
        jax.ShapeDtypeStruct((n_tok, D_SGU), _F32),
    ]
    return pl.pallas_call(
        _mix_sample_body,
        grid=(1,),
        in_specs=in_specs, out_specs=out_specs, out_shape=out_shape,
        compiler_params=pltpu.CompilerParams(
            dimension_semantics=("arbitrary",), vmem_limit_bytes=VMEM_LIMIT),
        name="mix_sample",
    )(x, sc_t, sp_t, *prm)


def kernel(x_prompt, x_sample, state_conv, state_pool, ffn1_norm, ffn1_w_gate_up, ffn1_w_down, mix_norm, w_in, conv_dw_w, conv_dw_b, conv_ln_g, conv_ln_b, w_conv_out, pool_w, pool_scale, w_pool_out, sgu_ln_g, sgu_ln_b, sgu_w, sgu_b, w_sgu_out, w_o, ffn2_norm, ffn2_w_gate_up, ffn2_w_down, final_norm):
    row = lambda p: p.reshape(DEPTH, 1, p.shape[-1])
    fnorm = final_norm.reshape(1, D_MODEL)
    ffn1_f32 = (ffn1_w_gate_up, ffn1_w_down)
    ffn2_f32 = (ffn2_w_gate_up, ffn2_w_down)
    mix_f32 = (w_in, w_conv_out, pool_w.reshape(DEPTH, D_POOL, LANES), w_pool_out, w_sgu_out, w_o)

    def mixer_params(mix_bf, conv_w, sgu_wt, sgu_bias):
        win_bf, wco_bf, pw_bf, wpo_bf, wso_bf, wo_bf = mix_bf
        return (row(mix_norm), win_bf, conv_w, row(conv_dw_b), row(conv_ln_g), row(conv_ln_b), wco_bf,
                pw_bf, row(pool_scale), wpo_bf, row(sgu_ln_g), row(sgu_ln_b), sgu_wt, sgu_bias, wso_bf, wo_bf)

    cw_slab = conv_dw_w.reshape(DEPTH, CONV_WIDTH, N_SLAB, LANES).transpose(0, 2, 1, 3)
    sb_full = jnp.repeat(jnp.swapaxes(sgu_b, 1, 2), LANES, axis=2)
    sw4 = jnp.tril(sgu_w[:, :, :DEC_SEQ, :DEC_SEQ])
    sw4 = jnp.repeat(jnp.transpose(sw4, (0, 2, 3, 1)), LANES, axis=3)
    sb4 = jnp.repeat(jnp.swapaxes(sgu_b[:, :, :DEC_SEQ], 1, 2), LANES, axis=2)

    xp = x_prompt.reshape(BATCH * SEQ, D_MODEL)
    xs = jnp.swapaxes(x_sample, 0, 1).reshape(DEC_SEQ * DEC_BATCH, D_MODEL)
    sc_t = jnp.swapaxes(state_conv, 1, 2)
    sp_t = jnp.swapaxes(state_pool, 1, 2)

    ffn1_bf = (ffn1_w_gate_up[0].astype(_BF), ffn1_w_down[0].astype(_BF))
    conv_p, pool_p, conv_s, pool_s, vn_s = [], [], [], [], []
    for layer in range(DEPTH):
        last = layer == DEPTH - 1
        xp, xs, mix_bf = _ffn(xp, xs, row(ffn1_norm), *ffn1_bf, fnorm, layer, False, mix_f32, layer)
        xp, cp, pp, ffn2_bf = _mix_prompt(
            xp, layer, mixer_params(mix_bf, cw_slab, sgu_w, sb_full), ffn2_f32)
        xs, cs, ps, vn = _mix_sample(xs, sc_t, sp_t, layer, mixer_params(mix_bf, conv_dw_w, sw4, sb4))
        xp, xs, ffn1_bf = _ffn(xp, xs, row(ffn2_norm), *ffn2_bf, fnorm, layer, last,
                               () if last else ffn1_f32, layer + 1)
        conv_p.append(cp)
        pool_p.append(pp)
        conv_s.append(jnp.swapaxes(cs, 0, 1))
        pool_s.append(jnp.swapaxes(ps, 0, 1))
        vn_s.append(jnp.swapaxes(vn.reshape(DEC_SEQ, DEC_BATCH, D_SGU), 0, 1))

    y_prompt = xp.reshape(BATCH, SEQ, D_MODEL)
    y_sample = jnp.swapaxes(xs.reshape(DEC_SEQ, DEC_BATCH, D_MODEL), 0, 1)
    return (y_prompt, y_sample, jnp.stack(conv_p), jnp.stack(conv_s),
            jnp.stack(pool_p), jnp.stack(pool_s), jnp.stack(vn_s))
```

```python
import functools

import jax
import jax.numpy as jnp
from jax import lax
from jax.experimental import pallas as pl
from jax.experimental.pallas import tpu as pltpu

D_MODEL = 1024
BATCH = 8
SEQ = 2048
DEPTH = 4
DEC_BATCH = 128
DEC_SEQ = 4
PAST_LEN = 16384
D_CONV = 512
CONV_WIDTH = 31
CONV_BUF = CONV_WIDTH - 1
D_POOL = 512
POOL_WINDOWS = (2, 4, 8, 16)
POOL_BUF = 15
D_SGU = 512
SGU_GROUPS = 4
CHUNK = 128
D_FF = 2816
D_IN = 5632
COL_P = 2 * D_CONV
COL_U = COL_P + D_POOL
COL_V = COL_U + D_SGU
COL_G = COL_V + D_SGU
RMS_EPS = 1e-6
LN_EPS = 1e-5

LANES = 128
N_SLAB = D_CONV // LANES
TM = 512
N_PROMPT_TILES = BATCH * SEQ // TM
CONV_HALO = 32
POOL_HALO = 16
ROW_BLK = 64
HALF_BLK = ROW_BLK // 2
FF_CHUNKS = ((0, 1024), (1024, 1024), (2048, 768))
ZIP_COLS = 512
ZIP_BLKS = 4
ZIP_LAG = 2
N_CVT = 16
VMEM_LIMIT = 56 * 1024 * 1024

_BF = jnp.bfloat16
_F32 = jnp.float32


def _dot(a, b):
    return jnp.dot(a, b, preferred_element_type=_F32)


def _rmsnorm(x, g):
    return x * lax.rsqrt(jnp.mean(x * x, axis=-1, keepdims=True) + RMS_EPS) * g


def _layernorm(x, g, b):
    mu = jnp.mean(x, axis=-1, keepdims=True)
    xc = x - mu
    var = jnp.mean(xc * xc, axis=-1, keepdims=True)
    return xc * lax.rsqrt(var + LN_EPS) * g + b


def _silu(x):
    return x * jax.nn.sigmoid(x)


def _exact_zero(v):
    u = pltpu.bitcast(v, jnp.uint32)
    u = lax.shift_right_logical(lax.shift_right_logical(u, jnp.uint32(16)), jnp.uint32(16))
    return pltpu.bitcast(u, _F32)


def _layer_spec(tail, layer):
    zeros = (0,) * len(tail)
    return pl.BlockSpec((None,) + tuple(tail), lambda *_: (layer,) + zeros,
                        pipeline_mode=pl.Buffered(1))


def _whole_spec(shape):
    zeros = (0,) * len(shape)
    return pl.BlockSpec(tuple(shape), lambda *_: zeros, pipeline_mode=pl.Buffered(1))


def _cvt_specs(stacked, layer, step_of):
    in_specs, out_specs, out_shape = [], [], []
    for w in stacked:
        _, rows, cols = w.shape
        blk = rows // N_CVT
        assert blk * N_CVT == rows and blk % 16 == 0, w.shape
        at = lambda *g: jnp.minimum(step_of(*g), N_CVT - 1)
        in_specs.append(pl.BlockSpec((None, blk, cols), lambda *g, at=at: (layer, at(*g), 0)))
        out_specs.append(pl.BlockSpec((blk, cols), lambda *g, at=at: (at(*g), 0)))
        out_shape.append(jax.ShapeDtypeStruct((rows, cols), _BF))
    return in_specs, out_specs, out_shape


def _round_weights(src_refs, dst_refs):
    for src, dst in zip(src_refs, dst_refs):
        dst[...] = src[...].astype(_BF)


def _ffn_tile(x, g_ref, wgu_ref, wd_ref, fn_ref, final):
    h = _rmsnorm(x, g_ref[...]).astype(_BF)
    acc = None
    for c0, cw in FF_CHUNKS:
        gate = _dot(h, wgu_ref[:, c0:c0 + cw])
        up = _dot(h, wgu_ref[:, D_FF + c0:D_FF + c0 + cw])
        part = _dot((_silu(gate) * up).astype(_BF), wd_ref[c0:c0 + cw, :])
        acc = part if acc is None else acc + part
    y = x + 0.5 * acc
    if final:
        y = _rmsnorm(y, fn_ref[...])
    return y


def _ffn_body(*refs, n_cvt, final):
    xp_ref, xs_ref, g_ref, wgu_ref, wd_ref, fn_ref = refs[:6]
    cvt_in = refs[6:6 + n_cvt]
    op_ref, os_ref = refs[6 + n_cvt:8 + n_cvt]
    cvt_out = refs[8 + n_cvt:]
    i = pl.program_id(0)

    @pl.when(i < N_PROMPT_TILES)
    def _():
        op_ref[...] = _ffn_tile(xp_ref[...], g_ref, wgu_ref, wd_ref, fn_ref, final)
        _round_weights(cvt_in, cvt_out)

    @pl.when(i == N_PROMPT_TILES)
    def _():
        os_ref[...] = _ffn_tile(xs_ref[...], g_ref, wgu_ref, wd_ref, fn_ref, final)


def _ffn(xp, xs, norm, wgu, wd, final_norm, layer, final, to_round, round_layer):
    last_p = N_PROMPT_TILES - 1
    cvt_in, cvt_out, cvt_shape = _cvt_specs(to_round, round_layer, lambda i: i)
    prompt_tile = pl.BlockSpec((TM, D_MODEL), lambda i: (jnp.minimum(i, last_p), 0))
    sample_tile = pl.BlockSpec((TM, D_MODEL), lambda i: (0, 0))
    out = pl.pallas_call(
        functools.partial(_ffn_body, n_cvt=len(to_round), final=final),
        grid=(N_PROMPT_TILES + 1,),
        in_specs=[
            prompt_tile,
            pl.BlockSpec((TM, D_MODEL), lambda i: (0, 0), pipeline_mode=pl.Buffered(1)),
            _layer_spec((1, D_MODEL), layer),
            _whole_spec((D_MODEL, 2 * D_FF)),
            _whole_spec((D_FF, D_MODEL)),
            pl.BlockSpec((1, D_MODEL), lambda i: (0, 0)),
        ] + cvt_in,
        out_specs=[prompt_tile, sample_tile] + cvt_out,
        out_shape=[jax.ShapeDtypeStruct(xp.shape, _F32), jax.ShapeDtypeStruct(xs.shape, _F32)] + cvt_shape,
        compiler_params=pltpu.CompilerParams(
            dimension_semantics=("arbitrary",), vmem_limit_bytes=VMEM_LIMIT),
        name="ffn",
    )(xp, xs, norm, wgu, wd, final_norm, *to_round)
    return out[0], out[1], tuple(out[2:])


def _pool_linear(m, pw_ref, ps_ref, wpo_ref):
    y = jnp.concatenate(
        [_dot(m[:, g * LANES:(g + 1) * LANES].astype(_BF), pw_ref[g * LANES:(g + 1) * LANES, :])
         for g in range(len(POOL_WINDOWS))], axis=1)
    return _dot((y * ps_ref[...]).astype(_BF), wpo_ref[...])


def _merge_out(x, gate_logits, wo_ref, y_a, y_b, y_c):
    gl = jax.nn.sigmoid(gate_logits)
    merged = (gl[:, 0:D_MODEL] * y_a + gl[:, D_MODEL:2 * D_MODEL] * y_b
              + gl[:, 2 * D_MODEL:3 * D_MODEL] * y_c)
    return x + _dot(merged.astype(_BF), wo_ref[...])


def _mix_prompt_body(*refs, n_cvt):
    (x_ref, g_ref, win_ref, cw_ref, cb_ref, clg_ref, clb_ref, wco_ref,
     pw_ref, ps_ref, wpo_ref, slg_ref, slb_ref, sw_ref, sb_ref, wso_ref, wo_ref) = refs[:17]
    cvt_in = refs[17:17 + n_cvt]
    xo_ref, convo_ref, poolo_ref = refs[17 + n_cvt:20 + n_cvt]
    cvt_out = refs[20 + n_cvt:20 + 2 * n_cvt]
    ah_ref, ph_ref, yc_ref, m_ref, h_ref = refs[20 + 2 * n_cvt:]
    s = pl.program_id(1)

    @pl.when(s == 0)
    def _():
        for g in range(N_SLAB):
            ah_ref[g, 0:CONV_HALO, :] = jnp.zeros((CONV_HALO, LANES), _F32)
            ph_ref[g, 0:POOL_HALO, :] = jnp.zeros((POOL_HALO, LANES), _F32)

    x = x_ref[...]
    h = _rmsnorm(x, g_ref[...]).astype(_BF)

    a2 = _dot(h, win_ref[:, 0:COL_P])
    a = a2[:, 0:D_CONV] * jax.nn.sigmoid(a2[:, D_CONV:2 * D_CONV])
    for g in range(N_SLAB):
        ah_ref[g, CONV_HALO:CONV_HALO + TM, :] = a[:, g * LANES:(g + 1) * LANES]

    h_ref[...] = h
    h_tile = h[0:16, 0:LANES]

    def piece(j, dep):
        if dep is not None:
            h_ref[0:16, 0:LANES] = h_tile + jnp.concatenate([dep, dep], axis=0).astype(_BF)
        c0 = COL_P + ZIP_COLS * j
        return _dot(h_ref[...], win_ref[:, c0:c0 + ZIP_COLS])

    def conv_group(i, dep):
        for b in range(i * ZIP_BLKS, (i + 1) * ZIP_BLKS):
            g = b // (TM // ROW_BLK)
            base = (b % (TM // ROW_BLK)) * ROW_BLK
            for ph in range(2):
                acc = (jnp.zeros((HALF_BLK, LANES), _F32) if dep is None
                       else jnp.concatenate([dep] * (HALF_BLK // 8), axis=0))
                for k in range(CONV_WIDTH):
                    start = base + (CONV_HALO - CONV_BUF) + k + ph
                    acc = acc + cw_ref[g, k:k + 1, :] * ah_ref[g, pl.ds(start, HALF_BLK, stride=2), :]
                yc_ref[g, pl.ds(base + ph, HALF_BLK, stride=2), :] = acc
        return _exact_zero(acc[0:8, :])

    n_piece = (D_IN - COL_P) // ZIP_COLS
    n_grp = N_SLAB * (TM // ROW_BLK) // ZIP_BLKS
    z_parts, conv_dep = [], []
    for j in range(n_piece):
        z_parts.append(piece(j, conv_dep[j - ZIP_LAG] if 0 <= j - ZIP_LAG < n_grp else None))
        if j < n_grp:
            conv_dep.append(conv_group(
                j, _exact_zero(z_parts[j - ZIP_LAG][0:8, 0:LANES]) if j >= ZIP_LAG else None))
    assert len(conv_dep) == n_grp
    z_rest = jnp.concatenate(z_parts, axis=1)

    yc = jnp.concatenate([yc_ref[g] for g in range(N_SLAB)], axis=1) + cb_ref[...]
    yc = _silu(_layernorm(yc, clg_ref[...], clb_ref[...]))
    y_a = _dot(yc.astype(_BF), wco_ref[...])

    p = z_rest[:, 0:COL_U - COL_P]
    for g in range(N_SLAB):
        ph_ref[g, POOL_HALO:POOL_HALO + TM, :] = p[:, g * LANES:(g + 1) * LANES]

    for i in range(TM // ROW_BLK):
        base = i * ROW_BLK
        for ph in range(2):
            pos = s * TM + base + ph + 2 * lax.broadcasted_iota(jnp.int32, (HALF_BLK, LANES), 0)
            for g, w in enumerate(POOL_WINDOWS):
                cur = ph_ref[g, pl.ds(base + POOL_HALO + ph, HALF_BLK, stride=2), :]
                tot = cur
                for j in range(1, w):
                    tot = tot + ph_ref[g, pl.ds(base + POOL_HALO + ph - j, HALF_BLK, stride=2), :]
                cnt = jnp.minimum(pos + 1, w).astype(_F32)
                m_ref[g, pl.ds(base + ph, HALF_BLK, stride=2), :] = tot / cnt - cur

    m = jnp.concatenate([m_ref[g] for g in range(N_SLAB)], axis=1)
    y_b = _pool_linear(m, pw_ref, ps_ref, wpo_ref)

    u = z_rest[:, COL_U - COL_P:COL_V - COL_P]
    v = z_rest[:, COL_V - COL_P:COL_G - COL_P]
    vn = _layernorm(v, slg_ref[...], slb_ref[...]).astype(_BF)
    n_chunk = TM // CHUNK
    row = lax.broadcasted_iota(jnp.int32, (CHUNK, CHUNK), 0)
    col = lax.broadcasted_iota(jnp.int32, (CHUNK, CHUNK), 1)
    mixed_groups = []
    for g in range(SGU_GROUPS):
        w_causal = jnp.where(row >= col, sw_ref[g], 0.0).astype(_BF)
        vg = vn[:, g * LANES:(g + 1) * LANES]
        rhs = jnp.concatenate([vg[c * CHUNK:(c + 1) * CHUNK, :] for c in range(n_chunk)], axis=1)
        res = _dot(w_causal, rhs)
        mixed_groups.append(
            jnp.concatenate([res[:, c * LANES:(c + 1) * LANES] for c in range(n_chunk)], axis=0))
    bias = jnp.concatenate([sb_ref[...]] * n_chunk, axis=0)
    mixed = jnp.concatenate(mixed_groups, axis=1) + bias
    y_c = _dot((u * mixed).astype(_BF), wso_ref[...])

    xo_ref[...] = _merge_out(x, z_rest[:, COL_G - COL_P:], wo_ref, y_a, y_b, y_c)

    for g in range(N_SLAB):
        convo_ref[:, g * LANES:(g + 1) * LANES] = ah_ref[g, CONV_HALO + TM - CONV_BUF:CONV_HALO + TM, :]
        poolo_ref[:, g * LANES:(g + 1) * LANES] = ph_ref[g, POOL_HALO + TM - POOL_BUF:POOL_HALO + TM, :]
        ah_ref[g, 0:CONV_HALO, :] = ah_ref[g, TM:TM + CONV_HALO, :]
        ph_ref[g, 0:POOL_HALO, :] = ph_ref[g, TM:TM + POOL_HALO, :]

    _round_weights(cvt_in, cvt_out)


def _mixer_weight_specs(layer, sgu_w_tail, sgu_b_tail, conv_w_tail):
    ls = lambda *tail: _layer_spec(tail, layer)
    return [
        ls(1, D_MODEL),
        _whole_spec((D_MODEL, D_IN)),
        ls(*conv_w_tail),
        ls(1, D_CONV), ls(1, D_CONV), ls(1, D_CONV),
        _whole_spec((D_CONV, D_MODEL)),
        _whole_spec((D_POOL, LANES)),
        ls(1, D_POOL),
        _whole_spec((D_POOL, D_MODEL)),
        ls(1, D_SGU), ls(1, D_SGU),
        ls(*sgu_w_tail),
        ls(*sgu_b_tail),
        _whole_spec((D_SGU, D_MODEL)),
        _whole_spec((D_MODEL, D_MODEL)),
    ]


def _mix_prompt(x, layer, prm, to_round):
    n_s = SEQ // TM
    tok = pl.BlockSpec((TM, D_MODEL), lambda b, s: (b * n_s + s, 0))
    cvt_in, cvt_out, cvt_shape = _cvt_specs(to_round, layer, lambda b, s: b * n_s + s)
    in_specs = [tok] + _mixer_weight_specs(
        layer, (SGU_GROUPS, CHUNK, CHUNK), (CHUNK, D_SGU), (N_SLAB, CONV_WIDTH, LANES)) + cvt_in
    out_specs = [
        tok,
        pl.BlockSpec((None, CONV_BUF, D_CONV), lambda b, s: (b, 0, 0)),
        pl.BlockSpec((None, POOL_BUF, D_POOL), lambda b, s: (b, 0, 0)),
    ] + cvt_out
    out_shape = [
        jax.ShapeDtypeStruct((BATCH * SEQ, D_MODEL), _F32),
        jax.ShapeDtypeStruct((BATCH, CONV_BUF, D_CONV), _F32),
        jax.ShapeDtypeStruct((BATCH, POOL_BUF, D_POOL), _F32),
    ] + cvt_shape
    scratch = [
        pltpu.VMEM((N_SLAB, CONV_HALO + TM, LANES), _F32),
        pltpu.VMEM((N_SLAB, POOL_HALO + TM, LANES), _F32),
        pltpu.VMEM((N_SLAB, TM, LANES), _F32),
        pltpu.VMEM((N_SLAB, TM, LANES), _F32),
        pltpu.VMEM((TM, D_MODEL), _BF),
    ]
    out = pl.pallas_call(
        functools.partial(_mix_prompt_body, n_cvt=len(to_round)),
        grid=(BATCH, n_s),
        in_specs=in_specs, out_specs=out_specs, out_shape=out_shape,
        scratch_shapes=scratch,
        compiler_params=pltpu.CompilerParams(
            dimension_semantics=("arbitrary", "arbitrary"), vmem_limit_bytes=VMEM_LIMIT),
        name="mix_prompt",
    )(x, *prm, *to_round)
    return out[0], out[1], out[2], tuple(out[3:])


def _mix_sample_body(x_ref, sc_ref, sp_ref, g_ref, win_ref, cw_ref, cb_ref, clg_ref, clb_ref, wco_ref,
                     pw_ref, ps_ref, wpo_ref, slg_ref, slb_ref, sw4_ref, sb4_ref, wso_ref, wo_ref,
                     xo_ref, convo_ref, poolo_ref, vn_ref):
    nb = DEC_BATCH
    x = x_ref[...]
    h = _rmsnorm(x, g_ref[...]).astype(_BF)

    a2 = _dot(h, win_ref[:, 0:COL_P])
    a = a2[:, 0:D_CONV] * jax.nn.sigmoid(a2[:, D_CONV:2 * D_CONV])
    a_t = [a[t * nb:(t + 1) * nb, :] for t in range(DEC_SEQ)]
    padded = lambda j: sc_ref[j] if j < CONV_BUF else a_t[j - CONV_BUF]
    ys = []
    for t in range(DEC_SEQ):
        acc = jnp.zeros((nb, D_CONV), _F32) + cb_ref[...]
        for k in range(CONV_WIDTH):
            acc = acc + cw_ref[k:k + 1, :] * padded(t + k)
        ys.append(acc)
    for j in range(CONV_BUF):
        convo_ref[j] = padded(j + DEC_SEQ)
    yc = _silu(_layernorm(jnp.concatenate(ys, axis=0), clg_ref[...], clb_ref[...]))
    y_a = _dot(yc.astype(_BF), wco_ref[...])

    p = _dot(h, win_ref[:, COL_P:COL_U])
    p_t = [p[t * nb:(t + 1) * nb, :] for t in range(DEC_SEQ)]
    ppad = lambda j: sp_ref[j] if j < POOL_BUF else p_t[j - POOL_BUF]
    ms = []
    for t in range(DEC_SEQ):
        parts = []
        for g, w in enumerate(POOL_WINDOWS):
            sl = slice(g * LANES, (g + 1) * LANES)
            tot = p_t[t][:, sl]
            for j in range(1, w):
                tot = tot + ppad(POOL_BUF + t - j)[:, sl]
            cnt = float(min(PAST_LEN + t + 1, w))
            parts.append(tot / cnt - p_t[t][:, sl])
        ms.append(jnp.concatenate(parts, axis=1))
    for j in range(POOL_BUF):
        poolo_ref[j] = ppad(j + DEC_SEQ)
    y_b = _pool_linear(jnp.concatenate(ms, axis=0), pw_ref, ps_ref, wpo_ref)

    u = _dot(h, win_ref[:, COL_U:COL_V])
    v = _dot(h, win_ref[:, COL_V:COL_G])
    vn = _layernorm(v, slg_ref[...], slb_ref[...])
    vn_ref[...] = vn
    vn_t = [vn[t * nb:(t + 1) * nb, :] for t in range(DEC_SEQ)]
    mixed = []
    for t in range(DEC_SEQ):
        acc = jnp.zeros((nb, D_SGU), _F32) + sb4_ref[t:t + 1, :]
        for src in range(t + 1):
            acc = acc + sw4_ref[t, src:src + 1, :] * vn_t[src]
        mixed.append(acc)
    y_c = _dot((u * jnp.concatenate(mixed, axis=0)).astype(_BF), wso_ref[...])

    xo_ref[...] = _merge_out(x, _dot(h, win_ref[:, COL_G:D_IN]), wo_ref, y_a, y_b, y_c)


def _mix_sample(x, sc_t, sp_t, layer, prm):
    n_tok = DEC_BATCH * DEC_SEQ
    in_specs = [
        pl.BlockSpec((n_tok, D_MODEL), lambda i: (0, 0)),
        _layer_spec((CONV_BUF, DEC_BATCH, D_CONV), layer),
        _layer_spec((POOL_BUF, DEC_BATCH, D_POOL), layer),
    ] + _mixer_weight_specs(layer, (DEC_SEQ, DEC_SEQ, D_SGU), (DEC_SEQ, D_SGU), (CONV_WIDTH, D_CONV))
    out_specs = [
        pl.BlockSpec((n_tok, D_MODEL), lambda i: (0, 0)),
        pl.BlockSpec((CONV_BUF, DEC_BATCH, D_CONV), lambda i: (0, 0, 0)),
        pl.BlockSpec((POOL_BUF, DEC_BATCH, D_POOL), lambda i: (0, 0, 0)),
        pl.BlockSpec((n_tok, D_SGU), lambda i: (0, 0)),
    ]
    out_shape = [
        jax.ShapeDtypeStruct((n_tok, D_MODEL), _F32),
        jax.ShapeDtypeStruct((CONV_BUF, DEC_BATCH, D_CONV), _F32),
        jax.ShapeDtypeStruct((POOL_BUF, DEC_BATCH, D_POOL), _F32),
        jax.ShapeDtypeStruct((n_tok, D_SGU), _F32),
    ]
    return pl.pallas_call(
        _mix_sample_body,
        grid=(1,),
        in_specs=in_specs, out_specs=out_specs, out_shape=out_shape,
        compiler_params=pltpu.CompilerParams(
            dimension_semantics=("arbitrary",), vmem_limit_bytes=VMEM_LIMIT),
        name="mix_sample",
    )(x, sc_t, sp_t, *prm)


def kernel(x_prompt, x_sample, state_conv, state_pool, ffn1_norm, ffn1_w_gate_up, ffn1_w_down, mix_norm, w_in, conv_dw_w, conv_dw_b, conv_ln_g, conv_ln_b, w_conv_out, pool_w, pool_scale, w_pool_out, sgu_ln_g, sgu_ln_b, sgu_w, sgu_b, w_sgu_out, w_o, ffn2_norm, ffn2_w_gate_up, ffn2_w_down, final_norm):
    row = lambda p: p.reshape(DEPTH, 1, p.shape[-1])
    fnorm = final_norm.reshape(1, D_MODEL)
    ffn1_f32 = (ffn1_w_gate_up, ffn1_w_down)
    ffn2_f32 = (ffn2_w_gate_up, ffn2_w_down)
    mix_f32 = (w_in, w_conv_out, pool_w.reshape(DEPTH, D_POOL, LANES), w_pool_out, w_sgu_out, w_o)

    def mixer_params(mix_bf, conv_w, sgu_wt, sgu_bias):
        win_bf, wco_bf, pw_bf, wpo_bf, wso_bf, wo_bf = mix_bf
        return (row(mix_norm), win_bf, conv_w, row(conv_dw_b), row(conv_ln_g), row(conv_ln_b), wco_bf,
                pw_bf, row(pool_scale), wpo_bf, row(sgu_ln_g), row(sgu_ln_b), sgu_wt, sgu_bias, wso_bf, wo_bf)

    cw_slab = conv_dw_w.reshape(DEPTH, CONV_WIDTH, N_SLAB, LANES).transpose(0, 2, 1, 3)
    sb_full = jnp.repeat(jnp.swapaxes(sgu_b, 1, 2), LANES, axis=2)
    sw4 = jnp.tril(sgu_w[:, :, :DEC_SEQ, :DEC_SEQ])
    sw4 = jnp.repeat(jnp.transpose(sw4, (0, 2, 3, 1)), LANES, axis=3)
    sb4 = jnp.repeat(jnp.swapaxes(sgu_b[:, :, :DEC_SEQ], 1, 2), LANES, axis=2)

    xp = x_prompt.reshape(BATCH * SEQ, D_MODEL)
    xs = jnp.swapaxes(x_sample, 0, 1).reshape(DEC_SEQ * DEC_BATCH, D_MODEL)
    sc_t = jnp.swapaxes(state_conv, 1, 2)
    sp_t = jnp.swapaxes(state_pool, 1, 2)

    ffn1_bf = (ffn1_w_gate_up[0].astype(_BF), ffn1_w_down[0].astype(_BF))
    conv_p, pool_p, conv_s, pool_s, vn_s = [], [], [], [], []
    for layer in range(DEPTH):
        last = layer == DEPTH - 1
        xp, xs, mix_bf = _ffn(xp, xs, row(ffn1_norm), *ffn1_bf, fnorm, layer, False, mix_f32, layer)
        xp, cp, pp, ffn2_bf = _mix_prompt(
            xp, layer, mixer_params(mix_bf, cw_slab, sgu_w, sb_full), ffn2_f32)
        xs, cs, ps, vn = _mix_sample(xs, sc_t, sp_t, layer, mixer_params(mix_bf, conv_dw_w, sw4, sb4))
        xp, xs, ffn1_bf = _ffn(xp, xs, row(ffn2_norm), *ffn2_bf, fnorm, layer, last,
                               () if last else ffn1_f32, layer + 1)
        conv_p.append(cp)
        pool_p.append(pp)
        conv_s.append(jnp.swapaxes(cs, 0, 1))
        pool_s.append(jnp.swapaxes(ps, 0, 1))
        vn_s.append(jnp.swapaxes(vn.reshape(DEC_SEQ, DEC_BATCH, D_SGU), 0, 1))

    y_prompt = xp.reshape(BATCH, SEQ, D_MODEL)
    y_sample = jnp.swapaxes(xs.reshape(DEC_SEQ, DEC_BATCH, D_MODEL), 0, 1)
    return (y_prompt, y_sample, jnp.stack(conv_p), jnp.stack(conv_s),
            jnp.stack(pool_p), jnp.stack(pool_s), jnp.stack(vn_s))
```

```python
import functools

import jax
import jax.numpy as jnp
from jax import lax
from jax.experimental import pallas as pl
from jax.experimental.pallas import tpu as pltpu

D_MODEL = 1024
BATCH = 8
SEQ = 2048
DEPTH = 4
DEC_BATCH = 128
DEC_SEQ = 4
PAST_LEN = 16384
D_CONV = 512
CONV_WIDTH = 31
CONV_BUF = CONV_WIDTH - 1
D_POOL = 512
POOL_WINDOWS = (2, 4, 8, 16)
POOL_BUF = 15
D_SGU = 512
SGU_GROUPS = 4
CHUNK = 128
D_FF = 2816
D_IN = 5632
COL_P = 2 * D_CONV
COL_U = COL_P + D_POOL
COL_V = COL_U + D_SGU
COL_G = COL_V + D_SGU
RMS_EPS = 1e-6
LN_EPS = 1e-5

LANES = 128
N_SLAB = D_CONV // LANES
TM = 512
TM_FFN = 1024
N_FFN_TILES = BATCH * SEQ // TM_FFN
SAMPLE_SPLIT = 2
CONV_HALO = 32
POOL_HALO = 16
ROW_BLK = 64
HALF_BLK = ROW_BLK // 2
FF_CHUNKS = ((0, 1024), (1024, 1024), (2048, 768))
ZIP_COLS = 512
ZIP_BLKS = 4
ZIP_LAG = 2
N_CVT = 16
VMEM_LIMIT = 56 * 1024 * 1024

_BF = jnp.bfloat16
_F32 = jnp.float32


def _dot(a, b):
    return jnp.dot(a, b, preferred_element_type=_F32)


def _rmsnorm(x, g):
    return x * lax.rsqrt(jnp.mean(x * x, axis=-1, keepdims=True) + RMS_EPS) * g


def _layernorm(x, g, b):
    mu = jnp.mean(x, axis=-1, keepdims=True)
    xc = x - mu
    var = jnp.mean(xc * xc, axis=-1, keepdims=True)
    return xc * lax.rsqrt(var + LN_EPS) * g + b


def _silu(x):
    return x * jax.nn.sigmoid(x)


def _exact_zero(v):
    u = pltpu.bitcast(v, jnp.uint32)
    u = lax.shift_right_logical(lax.shift_right_logical(u, jnp.uint32(16)), jnp.uint32(16))
    return pltpu.bitcast(u, _F32)


def _layer_spec(tail, layer):
    zeros = (0,) * len(tail)
    return pl.BlockSpec((None,) + tuple(tail), lambda *_: (layer,) + zeros,
                        pipeline_mode=pl.Buffered(1))


def _whole_spec(shape):
    zeros = (0,) * len(shape)
    return pl.BlockSpec(tuple(shape), lambda *_: zeros, pipeline_mode=pl.Buffered(1))


def _cvt_specs(stacked, layer, step_of):
    in_specs, out_specs, out_shape = [], [], []
    for w in stacked:
        _, rows, cols = w.shape
        blk = rows // N_CVT
        assert blk * N_CVT == rows and blk % 16 == 0, w.shape
        at = lambda *g: jnp.minimum(step_of(*g), N_CVT - 1)
        in_specs.append(pl.BlockSpec((None, blk, cols), lambda *g, at=at: (layer, at(*g), 0)))
        out_specs.append(pl.BlockSpec((blk, cols), lambda *g, at=at: (at(*g), 0)))
        out_shape.append(jax.ShapeDtypeStruct((rows, cols), _BF))
    return in_specs, out_specs, out_shape


def _round_weights(src_refs, dst_refs):
    for src, dst in zip(src_refs, dst_refs):
        dst[...] = src[...].astype(_BF)


def _ffn_tile(x, g_ref, wgu_ref, wd_ref, fn_ref, final):
    h = _rmsnorm(x, g_ref[...]).astype(_BF)
    acc = None
    for c0, cw in FF_CHUNKS:
        gate = _dot(h, wgu_ref[:, c0:c0 + cw])
        up = _dot(h, wgu_ref[:, D_FF + c0:D_FF + c0 + cw])
        part = _dot((_silu(gate) * up).astype(_BF), wd_ref[c0:c0 + cw, :])
        acc = part if acc is None else acc + part
    y = x + 0.5 * acc
    if final:
        y = _rmsnorm(y, fn_ref[...])
    return y


def _ffn_body(*refs, n_cvt, final):
    xp_ref, xs_ref, g_ref, wgu_ref, wd_ref, fn_ref = refs[:6]
    cvt_in = refs[6:6 + n_cvt]
    op_ref, os_ref = refs[6 + n_cvt:8 + n_cvt]
    cvt_out = refs[8 + n_cvt:]
    i = pl.program_id(0)

    @pl.when(i < N_FFN_TILES)
    def _():
        op_ref[...] = _ffn_tile(xp_ref[...], g_ref, wgu_ref, wd_ref, fn_ref, final)
        _round_weights(cvt_in, cvt_out)

    @pl.when(i == N_FFN_TILES)
    def _():
        os_ref[...] = _ffn_tile(xs_ref[...], g_ref, wgu_ref, wd_ref, fn_ref, final)


def _ffn(xp, xs, norm, wgu, wd, final_norm, layer, final, to_round, round_layer):
    last_p = N_FFN_TILES - 1
    n_sample = DEC_BATCH * DEC_SEQ
    cvt_in, cvt_out, cvt_shape = _cvt_specs(to_round, round_layer, lambda i: i)
    prompt_tile = pl.BlockSpec((TM_FFN, D_MODEL), lambda i: (jnp.minimum(i, last_p), 0))
    sample_tile = pl.BlockSpec((n_sample, D_MODEL), lambda i: (0, 0))
    out = pl.pallas_call(
        functools.partial(_ffn_body, n_cvt=len(to_round), final=final),
        grid=(N_FFN_TILES + 1,),
        in_specs=[
            prompt_tile,
            pl.BlockSpec((n_sample, D_MODEL), lambda i: (0, 0), pipeline_mode=pl.Buffered(1)),
            _layer_spec((1, D_MODEL), layer),
            _whole_spec((D_MODEL, 2 * D_FF)),
            _whole_spec((D_FF, D_MODEL)),
            pl.BlockSpec((1, D_MODEL), lambda i: (0, 0)),
        ] + cvt_in,
        out_specs=[prompt_tile, sample_tile] + cvt_out,
        out_shape=[jax.ShapeDtypeStruct(xp.shape, _F32), jax.ShapeDtypeStruct(xs.shape, _F32)] + cvt_shape,
        compiler_params=pltpu.CompilerParams(
            dimension_semantics=("arbitrary",), vmem_limit_bytes=VMEM_LIMIT),
        name="ffn",
    )(xp, xs, norm, wgu, wd, final_norm, *to_round)
    return out[0], out[1], tuple(out[2:])


def _pool_linear(m, pw_ref, ps_ref, wpo_ref):
    y = jnp.concatenate(
        [_dot(m[:, g * LANES:(g + 1) * LANES].astype(_BF), pw_ref[g * LANES:(g + 1) * LANES, :])
         for g in range(len(POOL_WINDOWS))], axis=1)
    return _dot((y * ps_ref[...]).astype(_BF), wpo_ref[...])


def _merge_out(x, gate_logits, wo_ref, y_a, y_b, y_c):
    gl = jax.nn.sigmoid(gate_logits)
    merged = (gl[:, 0:D_MODEL] * y_a + gl[:, D_MODEL:2 * D_MODEL] * y_b
              + gl[:, 2 * D_MODEL:3 * D_MODEL] * y_c)
    return x + _dot(merged.astype(_BF), wo_ref[...])


def _mix_prompt_body(*refs, n_cvt):
    (x_ref, g_ref, win_ref, cw_ref, cb_ref, clg_ref, clb_ref, wco_ref,
     pw_ref, ps_ref, wpo_ref, slg_ref, slb_ref, sw_ref, sb_ref, wso_ref, wo_ref) = refs[:17]
    cvt_in = refs[17:17 + n_cvt]
    xo_ref, convo_ref, poolo_ref = refs[17 + n_cvt:20 + n_cvt]
    cvt_out = refs[20 + n_cvt:20 + 2 * n_cvt]
    ah_ref, ph_ref, yc_ref, m_ref, h_ref = refs[20 + 2 * n_cvt:]
    s = pl.program_id(1)

    @pl.when(s == 0)
    def _():
        for g in range(N_SLAB):
            ah_ref[g, 0:CONV_HALO, :] = jnp.zeros((CONV_HALO, LANES), _F32)
            ph_ref[g, 0:POOL_HALO, :] = jnp.zeros((POOL_HALO, LANES), _F32)

    x = x_ref[...]
    h = _rmsnorm(x, g_ref[...]).astype(_BF)

    a2 = _dot(h, win_ref[:, 0:COL_P])
    a = a2[:, 0:D_CONV] * jax.nn.sigmoid(a2[:, D_CONV:2 * D_CONV])
    for g in range(N_SLAB):
        ah_ref[g, CONV_HALO:CONV_HALO + TM, :] = a[:, g * LANES:(g + 1) * LANES]

    h_ref[...] = h
    h_tile = h[0:16, 0:LANES]

    def piece(j, dep):
        if dep is not None:
            h_ref[0:16, 0:LANES] = h_tile + jnp.concatenate([dep, dep], axis=0).astype(_BF)
        c0 = COL_P + ZIP_COLS * j
        return _dot(h_ref[...], win_ref[:, c0:c0 + ZIP_COLS])

    def conv_group(i, dep):
        for b in range(i * ZIP_BLKS, (i + 1) * ZIP_BLKS):
            g = b // (TM // ROW_BLK)
            base = (b % (TM // ROW_BLK)) * ROW_BLK
            for ph in range(2):
                acc = (jnp.zeros((HALF_BLK, LANES), _F32) if dep is None
                       else jnp.concatenate([dep] * (HALF_BLK // 8), axis=0))
                for k in range(CONV_WIDTH):
                    start = base + (CONV_HALO - CONV_BUF) + k + ph
                    acc = acc + cw_ref[g, k:k + 1, :] * ah_ref[g, pl.ds(start, HALF_BLK, stride=2), :]
                yc_ref[g, pl.ds(base + ph, HALF_BLK, stride=2), :] = acc
        return _exact_zero(acc[0:8, :])

    n_piece = (D_IN - COL_P) // ZIP_COLS
    n_grp = N_SLAB * (TM // ROW_BLK) // ZIP_BLKS
    z_parts, conv_dep = [], []
    for j in range(n_piece):
        z_parts.append(piece(j, conv_dep[j - ZIP_LAG] if 0 <= j - ZIP_LAG < n_grp else None))
        if j < n_grp:
            conv_dep.append(conv_group(
                j, _exact_zero(z_parts[j - ZIP_LAG][0:8, 0:LANES]) if j >= ZIP_LAG else None))
    assert len(conv_dep) == n_grp
    z_rest = jnp.concatenate(z_parts, axis=1)

    yc = jnp.concatenate([yc_ref[g] for g in range(N_SLAB)], axis=1) + cb_ref[...]
    yc = _silu(_layernorm(yc, clg_ref[...], clb_ref[...]))
    y_a = _dot(yc.astype(_BF), wco_ref[...])

    p = z_rest[:, 0:COL_U - COL_P]
    for g in range(N_SLAB):
        ph_ref[g, POOL_HALO:POOL_HALO + TM, :] = p[:, g * LANES:(g + 1) * LANES]

    for i in range(TM // ROW_BLK):
        base = i * ROW_BLK
        for ph in range(2):
            pos = s * TM + base + ph + 2 * lax.broadcasted_iota(jnp.int32, (HALF_BLK, LANES), 0)
            for g, w in enumerate(POOL_WINDOWS):
                cur = ph_ref[g, pl.ds(base + POOL_HALO + ph, HALF_BLK, stride=2), :]
                tot = cur
                for j in range(1, w):
                    tot = tot + ph_ref[g, pl.ds(base + POOL_HALO + ph - j, HALF_BLK, stride=2), :]
                cnt = jnp.minimum(pos + 1, w).astype(_F32)
                m_ref[g, pl.ds(base + ph, HALF_BLK, stride=2), :] = tot / cnt - cur

    m = jnp.concatenate([m_ref[g] for g in range(N_SLAB)], axis=1)
    y_b = _pool_linear(m, pw_ref, ps_ref, wpo_ref)

    u = z_rest[:, COL_U - COL_P:COL_V - COL_P]
    v = z_rest[:, COL_V - COL_P:COL_G - COL_P]
    vn = _layernorm(v, slg_ref[...], slb_ref[...]).astype(_BF)
    n_chunk = TM // CHUNK
    row = lax.broadcasted_iota(jnp.int32, (CHUNK, CHUNK), 0)
    col = lax.broadcasted_iota(jnp.int32, (CHUNK, CHUNK), 1)
    mixed_groups = []
    for g in range(SGU_GROUPS):
        w_causal = jnp.where(row >= col, sw_ref[g], 0.0).astype(_BF)
        vg = vn[:, g * LANES:(g + 1) * LANES]
        rhs = jnp.concatenate([vg[c * CHUNK:(c + 1) * CHUNK, :] for c in range(n_chunk)], axis=1)
        res = _dot(w_causal, rhs)
        mixed_groups.append(
            jnp.concatenate([res[:, c * LANES:(c + 1) * LANES] for c in range(n_chunk)], axis=0))
    bias = jnp.concatenate([sb_ref[...]] * n_chunk, axis=0)
    mixed = jnp.concatenate(mixed_groups, axis=1) + bias
    y_c = _dot((u * mixed).astype(_BF), wso_ref[...])

    xo_ref[...] = _merge_out(x, z_rest[:, COL_G - COL_P:], wo_ref, y_a, y_b, y_c)

    for g in range(N_SLAB):
        convo_ref[:, g * LANES:(g + 1) * LANES] = ah_ref[g, CONV_HALO + TM - CONV_BUF:CONV_HALO + TM, :]
        poolo_ref[:, g * LANES:(g + 1) * LANES] = ph_ref[g, POOL_HALO + TM - POOL_BUF:POOL_HALO + TM, :]
        ah_ref[g, 0:CONV_HALO, :] = ah_ref[g, TM:TM + CONV_HALO, :]
        ph_ref[g, 0:POOL_HALO, :] = ph_ref[g, TM:TM + POOL_HALO, :]

    _round_weights(cvt_in, cvt_out)


def _mixer_weight_specs(layer, sgu_w_tail, sgu_b_tail, conv_w_tail):
    ls = lambda *tail: _layer_spec(tail, layer)
    return [
        ls(1, D_MODEL),
        _whole_spec((D_MODEL, D_IN)),
        ls(*conv_w_tail),
        ls(1, D_CONV), ls(1, D_CONV), ls(1, D_CONV),
        _whole_spec((D_CONV, D_MODEL)),
        _whole_spec((D_POOL, LANES)),
        ls(1, D_POOL),
        _whole_spec((D_POOL, D_MODEL)),
        ls(1, D_SGU), ls(1, D_SGU),
        ls(*sgu_w_tail),
        ls(*sgu_b_tail),
        _whole_spec((D_SGU, D_MODEL)),
        _whole_spec((D_MODEL, D_MODEL)),
    ]


def _mix_prompt(x, layer, prm, to_round):
    n_s = SEQ // TM
    tok = pl.BlockSpec((TM, D_MODEL), lambda b, s: (b * n_s + s, 0))
    cvt_in, cvt_out, cvt_shape = _cvt_specs(to_round, layer, lambda b, s: b * n_s + s)
    in_specs = [tok] + _mixer_weight_specs(
        layer, (SGU_GROUPS, CHUNK, CHUNK), (CHUNK, D_SGU), (N_SLAB, CONV_WIDTH, LANES)) + cvt_in
    out_specs = [
        tok,
        pl.BlockSpec((None, CONV_BUF, D_CONV), lambda b, s: (b, 0, 0)),
        pl.BlockSpec((None, POOL_BUF, D_POOL), lambda b, s: (b, 0, 0)),
    ] + cvt_out
    out_shape = [
        jax.ShapeDtypeStruct((BATCH * SEQ, D_MODEL), _F32),
        jax.ShapeDtypeStruct((BATCH, CONV_BUF, D_CONV), _F32),
        jax.ShapeDtypeStruct((BATCH, POOL_BUF, D_POOL), _F32),
    ] + cvt_shape
    scratch = [
        pltpu.VMEM((N_SLAB, CONV_HALO + TM, LANES), _F32),
        pltpu.VMEM((N_SLAB, POOL_HALO + TM, LANES), _F32),
        pltpu.VMEM((N_SLAB, TM, LANES), _F32),
        pltpu.VMEM((N_SLAB, TM, LANES), _F32),
        pltpu.VMEM((TM, D_MODEL), _BF),
    ]
    out = pl.pallas_call(
        functools.partial(_mix_prompt_body, n_cvt=len(to_round)),
        grid=(BATCH, n_s),
        in_specs=in_specs, out_specs=out_specs, out_shape=out_shape,
        scratch_shapes=scratch,
        compiler_params=pltpu.CompilerParams(
            dimension_semantics=("arbitrary", "arbitrary"), vmem_limit_bytes=VMEM_LIMIT),
        name="mix_prompt",
    )(x, *prm, *to_round)
    return out[0], out[1], out[2], tuple(out[3:])


def _mix_sample_body(x_ref, sc_ref, sp_ref, g_ref, win_ref, cw_ref, cb_ref, clg_ref, clb_ref, wco_ref,
                     pw_ref, ps_ref, wpo_ref, slg_ref, slb_ref, sw4_ref, sb4_ref, wso_ref, wo_ref,
                     xo_ref, convo_ref, poolo_ref, vn_ref):
    nb = DEC_BATCH // SAMPLE_SPLIT
    x = x_ref[...].reshape(DEC_SEQ * nb, D_MODEL)
    h = _rmsnorm(x, g_ref[...]).astype(_BF)

    a2 = _dot(h, win_ref[:, 0:COL_P])
    a = a2[:, 0:D_CONV] * jax.nn.sigmoid(a2[:, D_CONV:2 * D_CONV])
    a_t = [a[t * nb:(t + 1) * nb, :] for t in range(DEC_SEQ)]
    padded = lambda j: sc_ref[j] if j < CONV_BUF else a_t[j - CONV_BUF]
    ys = []
    for t in range(DEC_SEQ):
        acc = jnp.zeros((nb, D_CONV), _F32) + cb_ref[...]
        for k in range(CONV_WIDTH):
            acc = acc + cw_ref[k:k + 1, :] * padded(t + k)
        ys.append(acc)
    for j in range(CONV_BUF):
        convo_ref[j] = padded(j + DEC_SEQ)
    yc = _silu(_layernorm(jnp.concatenate(ys, axis=0), clg_ref[...], clb_ref[...]))
    y_a = _dot(yc.astype(_BF), wco_ref[...])

    p = _dot(h, win_ref[:, COL_P:COL_U])
    p_t = [p[t * nb:(t + 1) * nb, :] for t in range(DEC_SEQ)]
    ppad = lambda j: sp_ref[j] if j < POOL_BUF else p_t[j - POOL_BUF]
    ms = []
    for t in range(DEC_SEQ):
        parts = []
        for g, w in enumerate(POOL_WINDOWS):
            sl = slice(g * LANES, (g + 1) * LANES)
            tot = p_t[t][:, sl]
            for j in range(1, w):
                tot = tot + ppad(POOL_BUF + t - j)[:, sl]
            cnt = float(min(PAST_LEN + t + 1, w))
            parts.append(tot / cnt - p_t[t][:, sl])
        ms.append(jnp.concatenate(parts, axis=1))
    for j in range(POOL_BUF):
        poolo_ref[j] = ppad(j + DEC_SEQ)
    y_b = _pool_linear(jnp.concatenate(ms, axis=0), pw_ref, ps_ref, wpo_ref)

    u = _dot(h, win_ref[:, COL_U:COL_V])
    v = _dot(h, win_ref[:, COL_V:COL_G])
    vn = _layernorm(v, slg_ref[...], slb_ref[...])
    vn_ref[...] = vn.reshape(DEC_SEQ, nb, D_SGU)
    vn_t = [vn[t * nb:(t + 1) * nb, :] for t in range(DEC_SEQ)]
    mixed = []
    for t in range(DEC_SEQ):
        acc = jnp.zeros((nb, D_SGU), _F32) + sb4_ref[t:t + 1, :]
        for src in range(t + 1):
            acc = acc + sw4_ref[t, src:src + 1, :] * vn_t[src]
        mixed.append(acc)
    y_c = _dot((u * jnp.concatenate(mixed, axis=0)).astype(_BF), wso_ref[...])

    y = _merge_out(x, _dot(h, win_ref[:, COL_G:D_IN]), wo_ref, y_a, y_b, y_c)
    xo_ref[...] = y.reshape(DEC_SEQ, nb, D_MODEL)


def _mix_sample(x, sc_t, sp_t, layer, prm):
    nb = DEC_BATCH // SAMPLE_SPLIT
    seq_blk = lambda rows, width: pl.BlockSpec((rows, nb, width), lambda i: (0, i, 0))
    state_blk = lambda rows, width: pl.BlockSpec((None, rows, nb, width), lambda i: (layer, 0, i, 0))
    in_specs = [
        seq_blk(DEC_SEQ, D_MODEL),
        state_blk(CONV_BUF, D_CONV),
        state_blk(POOL_BUF, D_POOL),
    ] + _mixer_weight_specs(layer, (DEC_SEQ, DEC_SEQ, D_SGU), (DEC_SEQ, D_SGU), (CONV_WIDTH, D_CONV))
    out_specs = [
        seq_blk(DEC_SEQ, D_MODEL),
        seq_blk(CONV_BUF, D_CONV),
        seq_blk(POOL_BUF, D_POOL),
        seq_blk(DEC_SEQ, D_SGU),
    ]
    out_shape = [
        jax.ShapeDtypeStruct((DEC_SEQ, DEC_BATCH, D_MODEL), _F32),
        jax.ShapeDtypeStruct((CONV_BUF, DEC_BATCH, D_CONV), _F32),
        jax.ShapeDtypeStruct((POOL_BUF, DEC_BATCH, D_POOL), _F32),
        jax.ShapeDtypeStruct((DEC_SEQ, DEC_BATCH, D_SGU), _F32),
    ]
    return pl.pallas_call(
        _mix_sample_body,
        grid=(SAMPLE_SPLIT,),
        in_specs=in_specs, out_specs=out_specs, out_shape=out_shape,
        compiler_params=pltpu.CompilerParams(
            dimension_semantics=("arbitrary",), vmem_limit_bytes=VMEM_LIMIT),
        name="mix_sample",
    )(x, sc_t, sp_t, *prm)


def kernel(x_prompt, x_sample, state_conv, state_pool, ffn1_norm, ffn1_w_gate_up, ffn1_w_down, mix_norm, w_in, conv_dw_w, conv_dw_b, conv_ln_g, conv_ln_b, w_conv_out, pool_w, pool_scale, w_pool_out, sgu_ln_g, sgu_ln_b, sgu_w, sgu_b, w_sgu_out, w_o, ffn2_norm, ffn2_w_gate_up, ffn2_w_down, final_norm):
    row = lambda p: p.reshape(DEPTH, 1, p.shape[-1])
    fnorm = final_norm.reshape(1, D_MODEL)
    ffn1_f32 = (ffn1_w_gate_up, ffn1_w_down)
    ffn2_f32 = (ffn2_w_gate_up, ffn2_w_down)
    mix_f32 = (w_in, w_conv_out, pool_w.reshape(DEPTH, D_POOL, LANES), w_pool_out, w_sgu_out, w_o)

    def mixer_params(mix_bf, conv_w, sgu_wt, sgu_bias):
        win_bf, wco_bf, pw_bf, wpo_bf, wso_bf, wo_bf = mix_bf
        return (row(mix_norm), win_bf, conv_w, row(conv_dw_b), row(conv_ln_g), row(conv_ln_b), wco_bf,
                pw_bf, row(pool_scale), wpo_bf, row(sgu_ln_g), row(sgu_ln_b), sgu_wt, sgu_bias, wso_bf, wo_bf)

    cw_slab = conv_dw_w.reshape(DEPTH, CONV_WIDTH, N_SLAB, LANES).transpose(0, 2, 1, 3)
    sb_full = jnp.repeat(jnp.swapaxes(sgu_b, 1, 2), LANES, axis=2)
    sw4 = jnp.tril(sgu_w[:, :, :DEC_SEQ, :DEC_SEQ])
    sw4 = jnp.repeat(jnp.transpose(sw4, (0, 2, 3, 1)), LANES, axis=3)
    sb4 = jnp.repeat(jnp.swapaxes(sgu_b[:, :, :DEC_SEQ], 1, 2), LANES, axis=2)

    xp = x_prompt.reshape(BATCH * SEQ, D_MODEL)
    xs = jnp.swapaxes(x_sample, 0, 1).reshape(DEC_SEQ * DEC_BATCH, D_MODEL)
    sc_t = jnp.swapaxes(state_conv, 1, 2)
    sp_t = jnp.swapaxes(state_pool, 1, 2)

    ffn1_bf = (ffn1_w_gate_up[0].astype(_BF), ffn1_w_down[0].astype(_BF))
    conv_p, pool_p, conv_s, pool_s, vn_s = [], [], [], [], []
    for layer in range(DEPTH):
        last = layer == DEPTH - 1
        xp, xs, mix_bf = _ffn(xp, xs, row(ffn1_norm), *ffn1_bf, fnorm, layer, False, mix_f32, layer)
        xp, cp, pp, ffn2_bf = _mix_prompt(
            xp, layer, mixer_params(mix_bf, cw_slab, sgu_w, sb_full), ffn2_f32)
        xs, cs, ps, vn = _mix_sample(xs.reshape(DEC_SEQ, DEC_BATCH, D_MODEL), sc_t, sp_t, layer,
                                     mixer_params(mix_bf, conv_dw_w, sw4, sb4))
        xs = xs.reshape(DEC_SEQ * DEC_BATCH, D_MODEL)
        xp, xs, ffn1_bf = _ffn(xp, xs, row(ffn2_norm), *ffn2_bf, fnorm, layer, last,
                               () if last else ffn1_f32, layer + 1)
        conv_p.append(cp)
        pool_p.append(pp)
        conv_s.append(jnp.swapaxes(cs, 0, 1))
        pool_s.append(jnp.swapaxes(ps, 0, 1))
        vn_s.append(jnp.swapaxes(vn, 0, 1))

    y_prompt = xp.reshape(BATCH, SEQ, D_MODEL)
    y_sample = jnp.swapaxes(xs.reshape(DEC_SEQ, DEC_BATCH, D_MODEL), 0, 1)
    return (y_prompt, y_sample, jnp.stack(conv_p), jnp.stack(conv_s),
            jnp.stack(pool_p), jnp.stack(pool_s), jnp.stack(vn_s))
```

```python
import functools

import jax
import jax.numpy as jnp
from jax import lax
from jax.experimental import pallas as pl
from jax.experimental.pallas import tpu as pltpu

D_MODEL = 1024
BATCH = 8
SEQ = 2048
DEPTH = 4
DEC_BATCH = 128
DEC_SEQ = 4
PAST_LEN = 16384
D_CONV = 512
CONV_WIDTH = 31
CONV_BUF = CONV_WIDTH - 1
D_POOL = 512
POOL_WINDOWS = (2, 4, 8, 16)
POOL_BUF = 15
D_SGU = 512
SGU_GROUPS = 4
CHUNK = 128
D_FF = 2816
D_IN = 5632
COL_P = 2 * D_CONV
COL_U = COL_P + D_POOL
COL_V = COL_U + D_SGU
COL_G = COL_V + D_SGU
RMS_EPS = 1e-6
LN_EPS = 1e-5

LANES = 128
N_SLAB = D_CONV // LANES
TM = 512
TM_FFN = 1024
N_FFN_TILES = BATCH * SEQ // TM_FFN
SAMPLE_SPLIT = 2
CONV_HALO = 32
POOL_HALO = 16
ROW_BLK = 64
HALF_BLK = ROW_BLK // 2
FF_CHUNKS = ((0, 1024), (1024, 1024), (2048, 768))
ZIP_COLS = 512
ZIP_BLKS = 4
ZIP_LAG = 2
N_CVT = 16
N_CVT_SMALL = 8
CVT_SMALL_ELEMS = D_MODEL * D_MODEL
VMEM_LIMIT = 56 * 1024 * 1024

_BF = jnp.bfloat16
_F32 = jnp.float32


def _dot(a, b):
    return jnp.dot(a, b, preferred_element_type=_F32)


def _rmsnorm(x, g):
    return x * lax.rsqrt(jnp.mean(x * x, axis=-1, keepdims=True) + RMS_EPS) * g


def _layernorm(x, g, b):
    mu = jnp.mean(x, axis=-1, keepdims=True)
    xc = x - mu
    var = jnp.mean(xc * xc, axis=-1, keepdims=True)
    return xc * lax.rsqrt(var + LN_EPS) * g + b


def _silu(x):
    return x * jax.nn.sigmoid(x)


def _exact_zero(v):
    u = pltpu.bitcast(v, jnp.uint32)
    u = lax.shift_right_logical(lax.shift_right_logical(u, jnp.uint32(16)), jnp.uint32(16))
    return pltpu.bitcast(u, _F32)


def _layer_spec(tail, layer):
    zeros = (0,) * len(tail)
    return pl.BlockSpec((None,) + tuple(tail), lambda *_: (layer,) + zeros,
                        pipeline_mode=pl.Buffered(1))


def _whole_spec(shape):
    zeros = (0,) * len(shape)
    return pl.BlockSpec(tuple(shape), lambda *_: zeros, pipeline_mode=pl.Buffered(1))


def _cvt_specs(stacked, layer, step_of):
    in_specs, out_specs, out_shape = [], [], []
    for w in stacked:
        _, rows, cols = w.shape
        n_blk = N_CVT_SMALL if rows * cols <= CVT_SMALL_ELEMS else N_CVT
        blk = rows // n_blk
        assert blk * n_blk == rows and blk % 16 == 0, w.shape
        at = lambda *g, n_blk=n_blk: jnp.minimum(step_of(*g), n_blk - 1)
        in_specs.append(pl.BlockSpec((None, blk, cols), lambda *g, at=at: (layer, at(*g), 0)))
        out_specs.append(pl.BlockSpec((blk, cols), lambda *g, at=at: (at(*g), 0)))
        out_shape.append(jax.ShapeDtypeStruct((rows, cols), _BF))
    return in_specs, out_specs, out_shape


def _round_weights(src_refs, dst_refs):
    for src, dst in zip(src_refs, dst_refs):
        dst[...] = src[...].astype(_BF)


def _ffn_tile(x, g_ref, wgu_ref, wd_ref, fn_ref, final):
    h = _rmsnorm(x, g_ref[...]).astype(_BF)
    acc = None
    for c0, cw in FF_CHUNKS:
        gate = _dot(h, wgu_ref[:, c0:c0 + cw])
        up = _dot(h, wgu_ref[:, D_FF + c0:D_FF + c0 + cw])
        part = _dot((_silu(gate) * up).astype(_BF), wd_ref[c0:c0 + cw, :])
        acc = part if acc is None else acc + part
    y = x + 0.5 * acc
    if final:
        y = _rmsnorm(y, fn_ref[...])
    return y


def _ffn_body(*refs, n_cvt, final):
    xp_ref, xs_ref, g_ref, wgu_ref, wd_ref, fn_ref = refs[:6]
    cvt_in = refs[6:6 + n_cvt]
    op_ref, os_ref = refs[6 + n_cvt:8 + n_cvt]
    cvt_out = refs[8 + n_cvt:]
    i = pl.program_id(0)

    @pl.when(i < N_FFN_TILES)
    def _():
        op_ref[...] = _ffn_tile(xp_ref[...], g_ref, wgu_ref, wd_ref, fn_ref, final)
        _round_weights(cvt_in, cvt_out)

    @pl.when(i == N_FFN_TILES)
    def _():
        os_ref[...] = _ffn_tile(xs_ref[...], g_ref, wgu_ref, wd_ref, fn_ref, final)


def _ffn(xp, xs, norm, wgu, wd, final_norm, layer, final, to_round, round_layer):
    last_p = N_FFN_TILES - 1
    n_sample = DEC_BATCH * DEC_SEQ
    cvt_in, cvt_out, cvt_shape = _cvt_specs(to_round, round_layer, lambda i: i)
    prompt_tile = pl.BlockSpec((TM_FFN, D_MODEL), lambda i: (jnp.minimum(i, last_p), 0))
    sample_tile = pl.BlockSpec((n_sample, D_MODEL), lambda i: (0, 0))
    out = pl.pallas_call(
        functools.partial(_ffn_body, n_cvt=len(to_round), final=final),
        grid=(N_FFN_TILES + 1,),
        in_specs=[
            prompt_tile,
            pl.BlockSpec((n_sample, D_MODEL), lambda i: (0, 0), pipeline_mode=pl.Buffered(1)),
            _layer_spec((1, D_MODEL), layer),
            _whole_spec((D_MODEL, 2 * D_FF)),
            _whole_spec((D_FF, D_MODEL)),
            pl.BlockSpec((1, D_MODEL), lambda i: (0, 0)),
        ] + cvt_in,
        out_specs=[prompt_tile, sample_tile] + cvt_out,
        out_shape=[jax.ShapeDtypeStruct(xp.shape, _F32), jax.ShapeDtypeStruct(xs.shape, _F32)] + cvt_shape,
        compiler_params=pltpu.CompilerParams(
            dimension_semantics=("arbitrary",), vmem_limit_bytes=VMEM_LIMIT),
        name="ffn",
    )(xp, xs, norm, wgu, wd, final_norm, *to_round)
    return out[0], out[1], tuple(out[2:])


def _pool_linear(m, pw_ref, ps_ref, wpo_ref):
    y = jnp.concatenate(
        [_dot(m[:, g * LANES:(g + 1) * LANES].astype(_BF), pw_ref[g * LANES:(g + 1) * LANES, :])
         for g in range(len(POOL_WINDOWS))], axis=1)
    return _dot((y * ps_ref[...]).astype(_BF), wpo_ref[...])


def _merge_out(x, gate_logits, wo_ref, y_a, y_b, y_c):
    gl = jax.nn.sigmoid(gate_logits)
    merged = (gl[:, 0:D_MODEL] * y_a + gl[:, D_MODEL:2 * D_MODEL] * y_b
              + gl[:, 2 * D_MODEL:3 * D_MODEL] * y_c)
    return x + _dot(merged.astype(_BF), wo_ref[...])


def _mix_prompt_body(*refs, n_cvt):
    (x_ref, g_ref, win_ref, cw_ref, cb_ref, clg_ref, clb_ref, wco_ref,
     pw_ref, ps_ref, wpo_ref, slg_ref, slb_ref, sw_ref, sb_ref, wso_ref, wo_ref) = refs[:17]
    cvt_in = refs[17:17 + n_cvt]
    xo_ref, convo_ref, poolo_ref = refs[17 + n_cvt:20 + n_cvt]
    cvt_out = refs[20 + n_cvt:20 + 2 * n_cvt]
    ah_ref, ph_ref, yc_ref, m_ref, h_ref = refs[20 + 2 * n_cvt:]
    s = pl.program_id(1)

    @pl.when(s == 0)
    def _():
        for g in range(N_SLAB):
            ah_ref[g, 0:CONV_HALO, :] = jnp.zeros((CONV_HALO, LANES), _F32)
            ph_ref[g, 0:POOL_HALO, :] = jnp.zeros((POOL_HALO, LANES), _F32)

    x = x_ref[...]
    h = _rmsnorm(x, g_ref[...]).astype(_BF)

    a2 = _dot(h, win_ref[:, 0:COL_P])
    a = a2[:, 0:D_CONV] * jax.nn.sigmoid(a2[:, D_CONV:2 * D_CONV])
    for g in range(N_SLAB):
        ah_ref[g, CONV_HALO:CONV_HALO + TM, :] = a[:, g * LANES:(g + 1) * LANES]

    h_ref[...] = h
    h_tile = h[0:16, 0:LANES]

    def piece(j, dep):
        if dep is not None:
            h_ref[0:16, 0:LANES] = h_tile + jnp.concatenate([dep, dep], axis=0).astype(_BF)
        c0 = COL_P + ZIP_COLS * j
        return _dot(h_ref[...], win_ref[:, c0:c0 + ZIP_COLS])

    def conv_group(i, dep):
        for b in range(i * ZIP_BLKS, (i + 1) * ZIP_BLKS):
            g = b // (TM // ROW_BLK)
            base = (b % (TM // ROW_BLK)) * ROW_BLK
            for ph in range(2):
                acc = (jnp.zeros((HALF_BLK, LANES), _F32) if dep is None
                       else jnp.concatenate([dep] * (HALF_BLK // 8), axis=0))
                for k in range(CONV_WIDTH):
                    start = base + (CONV_HALO - CONV_BUF) + k + ph
                    acc = acc + cw_ref[g, k:k + 1, :] * ah_ref[g, pl.ds(start, HALF_BLK, stride=2), :]
                yc_ref[g, pl.ds(base + ph, HALF_BLK, stride=2), :] = acc
        return _exact_zero(acc[0:8, :])

    n_piece = (D_IN - COL_P) // ZIP_COLS
    n_grp = N_SLAB * (TM // ROW_BLK) // ZIP_BLKS
    z_parts, conv_dep = [], []
    for j in range(n_piece):
        z_parts.append(piece(j, conv_dep[j - ZIP_LAG] if 0 <= j - ZIP_LAG < n_grp else None))
        if j < n_grp:
            conv_dep.append(conv_group(
                j, _exact_zero(z_parts[j - ZIP_LAG][0:8, 0:LANES]) if j >= ZIP_LAG else None))
    assert len(conv_dep) == n_grp
    z_rest = jnp.concatenate(z_parts, axis=1)

    yc = jnp.concatenate([yc_ref[g] for g in range(N_SLAB)], axis=1) + cb_ref[...]
    yc = _silu(_layernorm(yc, clg_ref[...], clb_ref[...]))
    y_a = _dot(yc.astype(_BF), wco_ref[...])

    p = z_rest[:, 0:COL_U - COL_P]
    for g in range(N_SLAB):
        ph_ref[g, POOL_HALO:POOL_HALO + TM, :] = p[:, g * LANES:(g + 1) * LANES]

    for i in range(TM // ROW_BLK):
        base = i * ROW_BLK
        for ph in range(2):
            pos = s * TM + base + ph + 2 * lax.broadcasted_iota(jnp.int32, (HALF_BLK, LANES), 0)
            for g, w in enumerate(POOL_WINDOWS):
                cur = ph_ref[g, pl.ds(base + POOL_HALO + ph, HALF_BLK, stride=2), :]
                tot = cur
                for j in range(1, w):
                    tot = tot + ph_ref[g, pl.ds(base + POOL_HALO + ph - j, HALF_BLK, stride=2), :]
                cnt = jnp.minimum(pos + 1, w).astype(_F32)
                m_ref[g, pl.ds(base + ph, HALF_BLK, stride=2), :] = tot / cnt - cur

    m = jnp.concatenate([m_ref[g] for g in range(N_SLAB)], axis=1)
    y_b = _pool_linear(m, pw_ref, ps_ref, wpo_ref)

    u = z_rest[:, COL_U - COL_P:COL_V - COL_P]
    v = z_rest[:, COL_V - COL_P:COL_G - COL_P]
    vn = _layernorm(v, slg_ref[...], slb_ref[...]).astype(_BF)
    n_chunk = TM // CHUNK
    row = lax.broadcasted_iota(jnp.int32, (CHUNK, CHUNK), 0)
    col = lax.broadcasted_iota(jnp.int32, (CHUNK, CHUNK), 1)
    mixed_groups = []
    for g in range(SGU_GROUPS):
        w_causal = jnp.where(row >= col, sw_ref[g], 0.0).astype(_BF)
        vg = vn[:, g * LANES:(g + 1) * LANES]
        rhs = jnp.concatenate([vg[c * CHUNK:(c + 1) * CHUNK, :] for c in range(n_chunk)], axis=1)
        res = _dot(w_causal, rhs)
        mixed_groups.append(
            jnp.concatenate([res[:, c * LANES:(c + 1) * LANES] for c in range(n_chunk)], axis=0))
    bias = jnp.concatenate([sb_ref[...]] * n_chunk, axis=0)
    mixed = jnp.concatenate(mixed_groups, axis=1) + bias
    y_c = _dot((u * mixed).astype(_BF), wso_ref[...])

    xo_ref[...] = _merge_out(x, z_rest[:, COL_G - COL_P:], wo_ref, y_a, y_b, y_c)

    for g in range(N_SLAB):
        convo_ref[:, g * LANES:(g + 1) * LANES] = ah_ref[g, CONV_HALO + TM - CONV_BUF:CONV_HALO + TM, :]
        poolo_ref[:, g * LANES:(g + 1) * LANES] = ph_ref[g, POOL_HALO + TM - POOL_BUF:POOL_HALO + TM, :]
        ah_ref[g, 0:CONV_HALO, :] = ah_ref[g, TM:TM + CONV_HALO, :]
        ph_ref[g, 0:POOL_HALO, :] = ph_ref[g, TM:TM + POOL_HALO, :]

    _round_weights(cvt_in, cvt_out)


def _mixer_weight_specs(layer, sgu_w_tail, sgu_b_tail, conv_w_tail):
    ls = lambda *tail: _layer_spec(tail, layer)
    return [
        ls(1, D_MODEL),
        _whole_spec((D_MODEL, D_IN)),
        ls(*conv_w_tail),
        ls(1, D_CONV), ls(1, D_CONV), ls(1, D_CONV),
        _whole_spec((D_CONV, D_MODEL)),
        _whole_spec((D_POOL, LANES)),
        ls(1, D_POOL),
        _whole_spec((D_POOL, D_MODEL)),
        ls(1, D_SGU), ls(1, D_SGU),
        ls(*sgu_w_tail),
        ls(*sgu_b_tail),
        _whole_spec((D_SGU, D_MODEL)),
        _whole_spec((D_MODEL, D_MODEL)),
    ]


def _mix_prompt(x, layer, prm, to_round):
    n_s = SEQ // TM
    tok = pl.BlockSpec((TM, D_MODEL), lambda b, s: (b * n_s + s, 0))
    cvt_in, cvt_out, cvt_shape = _cvt_specs(to_round, layer, lambda b, s: b * n_s + s)
    in_specs = [tok] + _mixer_weight_specs(
        layer, (SGU_GROUPS, CHUNK, CHUNK), (CHUNK, D_SGU), (N_SLAB, CONV_WIDTH, LANES)) + cvt_in
    out_specs = [
        tok,
        pl.BlockSpec((None, CONV_BUF, D_CONV), lambda b, s: (b, 0, 0)),
        pl.BlockSpec((None, POOL_BUF, D_POOL), lambda b, s: (b, 0, 0)),
    ] + cvt_out
    out_shape = [
        jax.ShapeDtypeStruct((BATCH * SEQ, D_MODEL), _F32),
        jax.ShapeDtypeStruct((BATCH, CONV_BUF, D_CONV), _F32),
        jax.ShapeDtypeStruct((BATCH, POOL_BUF, D_POOL), _F32),
    ] + cvt_shape
    scratch = [
        pltpu.VMEM((N_SLAB, CONV_HALO + TM, LANES), _F32),
        pltpu.VMEM((N_SLAB, POOL_HALO + TM, LANES), _F32),
        pltpu.VMEM((N_SLAB, TM, LANES), _F32),
        pltpu.VMEM((N_SLAB, TM, LANES), _F32),
        pltpu.VMEM((TM, D_MODEL), _BF),
    ]
    out = pl.pallas_call(
        functools.partial(_mix_prompt_body, n_cvt=len(to_round)),
        grid=(BATCH, n_s),
        in_specs=in_specs, out_specs=out_specs, out_shape=out_shape,
        scratch_shapes=scratch,
        compiler_params=pltpu.CompilerParams(
            dimension_semantics=("arbitrary", "arbitrary"), vmem_limit_bytes=VMEM_LIMIT),
        name="mix_prompt",
    )(x, *prm, *to_round)
    return out[0], out[1], out[2], tuple(out[3:])


def _mix_sample_body(x_ref, sc_ref, sp_ref, g_ref, win_ref, cw_ref, cb_ref, clg_ref, clb_ref, wco_ref,
                     pw_ref, ps_ref, wpo_ref, slg_ref, slb_ref, sw4_ref, sb4_ref, wso_ref, wo_ref,
                     xo_ref, convo_ref, poolo_ref, vn_ref):
    nb = DEC_BATCH // SAMPLE_SPLIT
    x = x_ref[...].reshape(DEC_SEQ * nb, D_MODEL)
    h = _rmsnorm(x, g_ref[...]).astype(_BF)

    a2 = _dot(h, win_ref[:, 0:COL_P])
    a = a2[:, 0:D_CONV] * jax.nn.sigmoid(a2[:, D_CONV:2 * D_CONV])
    a_t = [a[t * nb:(t + 1) * nb, :] for t in range(DEC_SEQ)]
    padded = lambda j: sc_ref[j] if j < CONV_BUF else a_t[j - CONV_BUF]
    ys = []
    for t in range(DEC_SEQ):
        acc = jnp.zeros((nb, D_CONV), _F32) + cb_ref[...]
        for k in range(CONV_WIDTH):
            acc = acc + cw_ref[k:k + 1, :] * padded(t + k)
        ys.append(acc)
    for j in range(CONV_BUF):
        convo_ref[j] = padded(j + DEC_SEQ)
    yc = _silu(_layernorm(jnp.concatenate(ys, axis=0), clg_ref[...], clb_ref[...]))
    y_a = _dot(yc.astype(_BF), wco_ref[...])

    p = _dot(h, win_ref[:, COL_P:COL_U])
    p_t = [p[t * nb:(t + 1) * nb, :] for t in range(DEC_SEQ)]
    ppad = lambda j: sp_ref[j] if j < POOL_BUF else p_t[j - POOL_BUF]
    ms = []
    for t in range(DEC_SEQ):
        parts = []
        for g, w in enumerate(POOL_WINDOWS):
            sl = slice(g * LANES, (g + 1) * LANES)
            tot = p_t[t][:, sl]
            for j in range(1, w):
                tot = tot + ppad(POOL_BUF + t - j)[:, sl]
            cnt = float(min(PAST_LEN + t + 1, w))
            parts.append(tot / cnt - p_t[t][:, sl])
        ms.append(jnp.concatenate(parts, axis=1))
    for j in range(POOL_BUF):
        poolo_ref[j] = ppad(j + DEC_SEQ)
    y_b = _pool_linear(jnp.concatenate(ms, axis=0), pw_ref, ps_ref, wpo_ref)

    u = _dot(h, win_ref[:, COL_U:COL_V])
    v = _dot(h, win_ref[:, COL_V:COL_G])
    vn = _layernorm(v, slg_ref[...], slb_ref[...])
    vn_ref[...] = vn.reshape(DEC_SEQ, nb, D_SGU)
    vn_t = [vn[t * nb:(t + 1) * nb, :] for t in range(DEC_SEQ)]
    mixed = []
    for t in range(DEC_SEQ):
        acc = jnp.zeros((nb, D_SGU), _F32) + sb4_ref[t:t + 1, :]
        for src in range(t + 1):
            acc = acc + sw4_ref[t, src:src + 1, :] * vn_t[src]
        mixed.append(acc)
    y_c = _dot((u * jnp.concatenate(mixed, axis=0)).astype(_BF), wso_ref[...])

    y = _merge_out(x, _dot(h, win_ref[:, COL_G:D_IN]), wo_ref, y_a, y_b, y_c)
    xo_ref[...] = y.reshape(DEC_SEQ, nb, D_MODEL)


def _mix_sample(x, sc_t, sp_t, layer, prm):
    nb = DEC_BATCH // SAMPLE_SPLIT
    seq_blk = lambda rows, width: pl.BlockSpec((rows, nb, width), lambda i: (0, i, 0))
    state_blk = lambda rows, width: pl.BlockSpec((None, rows, nb, width), lambda i: (layer, 0, i, 0))
    in_specs = [
        seq_blk(DEC_SEQ, D_MODEL),
        state_blk(CONV_BUF, D_CONV),
        state_blk(POOL_BUF, D_POOL),
    ] + _mixer_weight_specs(layer, (DEC_SEQ, DEC_SEQ, D_SGU), (DEC_SEQ, D_SGU), (CONV_WIDTH, D_CONV))
    out_specs = [
        seq_blk(DEC_SEQ, D_MODEL),
        seq_blk(CONV_BUF, D_CONV),
        seq_blk(POOL_BUF, D_POOL),
        seq_blk(DEC_SEQ, D_SGU),
    ]
    out_shape = [
        jax.ShapeDtypeStruct((DEC_SEQ, DEC_BATCH, D_MODEL), _F32),
        jax.ShapeDtypeStruct((CONV_BUF, DEC_BATCH, D_CONV), _F32),
        jax.ShapeDtypeStruct((POOL_BUF, DEC_BATCH, D_POOL), _F32),
        jax.ShapeDtypeStruct((DEC_SEQ, DEC_BATCH, D_SGU), _F32),
    ]
    return pl.pallas_call(
        _mix_sample_body,
        grid=(SAMPLE_SPLIT,),
        in_specs=in_specs, out_specs=out_specs, out_shape=out_shape,
        compiler_params=pltpu.CompilerParams(
            dimension_semantics=("arbitrary",), vmem_limit_bytes=VMEM_LIMIT),
        name="mix_sample",
    )(x, sc_t, sp_t, *prm)


def kernel(x_prompt, x_sample, state_conv, state_pool, ffn1_norm, ffn1_w_gate_up, ffn1_w_down, mix_norm, w_in, conv_dw_w, conv_dw_b, conv_ln_g, conv_ln_b, w_conv_out, pool_w, pool_scale, w_pool_out, sgu_ln_g, sgu_ln_b, sgu_w, sgu_b, w_sgu_out, w_o, ffn2_norm, ffn2_w_gate_up, ffn2_w_down, final_norm):
    row = lambda p: p.reshape(DEPTH, 1, p.shape[-1])
    fnorm = final_norm.reshape(1, D_MODEL)
    ffn1_f32 = (ffn1_w_gate_up, ffn1_w_down)
    ffn2_f32 = (ffn2_w_gate_up, ffn2_w_down)
    mix_f32 = (w_in, w_conv_out, pool_w.reshape(DEPTH, D_POOL, LANES), w_pool_out, w_sgu_out, w_o)

    def mixer_params(mix_bf, conv_w, sgu_wt, sgu_bias):
        win_bf, wco_bf, pw_bf, wpo_bf, wso_bf, wo_bf = mix_bf
        return (row(mix_norm), win_bf, conv_w, row(conv_dw_b), row(conv_ln_g), row(conv_ln_b), wco_bf,
                pw_bf, row(pool_scale), wpo_bf, row(sgu_ln_g), row(sgu_ln_b), sgu_wt, sgu_bias, wso_bf, wo_bf)

    cw_slab = conv_dw_w.reshape(DEPTH, CONV_WIDTH, N_SLAB, LANES).transpose(0, 2, 1, 3)
    sb_full = jnp.repeat(jnp.swapaxes(sgu_b, 1, 2), LANES, axis=2)
    sw4 = jnp.tril(sgu_w[:, :, :DEC_SEQ, :DEC_SEQ])
    sw4 = jnp.repeat(jnp.transpose(sw4, (0, 2, 3, 1)), LANES, axis=3)
    sb4 = jnp.repeat(jnp.swapaxes(sgu_b[:, :, :DEC_SEQ], 1, 2), LANES, axis=2)

    xp = x_prompt.reshape(BATCH * SEQ, D_MODEL)
    xs = jnp.swapaxes(x_sample, 0, 1).reshape(DEC_SEQ * DEC_BATCH, D_MODEL)
    sc_t = jnp.swapaxes(state_conv, 1, 2)
    sp_t = jnp.swapaxes(state_pool, 1, 2)

    ffn1_bf = (ffn1_w_gate_up[0].astype(_BF), ffn1_w_down[0].astype(_BF))
    conv_p, pool_p, conv_s, pool_s, vn_s = [], [], [], [], []
    for layer in range(DEPTH):
        last = layer == DEPTH - 1
        xp, xs, mix_bf = _ffn(xp, xs, row(ffn1_norm), *ffn1_bf, fnorm, layer, False, mix_f32, layer)
        xp, cp, pp, ffn2_bf = _mix_prompt(
            xp, layer, mixer_params(mix_bf, cw_slab, sgu_w, sb_full), ffn2_f32)
        xs, cs, ps, vn = _mix_sample(xs.reshape(DEC_SEQ, DEC_BATCH, D_MODEL), sc_t, sp_t, layer,
                                     mixer_params(mix_bf, conv_dw_w, sw4, sb4))
        xs = xs.reshape(DEC_SEQ * DEC_BATCH, D_MODEL)
        xp, xs, ffn1_bf = _ffn(xp, xs, row(ffn2_norm), *ffn2_bf, fnorm, layer, last,
                               () if last else ffn1_f32, layer + 1)
        conv_p.append(cp)
        pool_p.append(pp)
        conv_s.append(jnp.swapaxes(cs, 0, 1))
        pool_s.append(jnp.swapaxes(ps, 0, 1))
        vn_s.append(jnp.swapaxes(vn, 0, 1))

    y_prompt = xp.reshape(BATCH, SEQ, D_MODEL)
    y_sample = jnp.swapaxes(xs.reshape(DEC_SEQ, DEC_BATCH, D_MODEL), 0, 1)
    return (y_prompt, y_sample, jnp.stack(conv_p), jnp.stack(conv_s),
            jnp.stack(pool_p), jnp.stack(pool_s), jnp.stack(vn_s))
```

```python
import functools

import jax
import jax.numpy as jnp
from jax import lax
from jax.experimental import pallas as pl
from jax.experimental.pallas import tpu as pltpu

D_MODEL = 1024
BATCH = 8
SEQ = 2048
DEPTH = 4
DEC_BATCH = 128
DEC_SEQ = 4
PAST_LEN = 16384
D_CONV = 512
CONV_WIDTH = 31
CONV_BUF = CONV_WIDTH - 1
D_POOL = 512
POOL_WINDOWS = (2, 4, 8, 16)
POOL_BUF = 15
D_SGU = 512
SGU_GROUPS = 4
CHUNK = 128
D_FF = 2816
D_IN = 5632
COL_P = 2 * D_CONV
COL_U = COL_P + D_POOL
COL_V = COL_U + D_SGU
COL_G = COL_V + D_SGU
RMS_EPS = 1e-6
LN_EPS = 1e-5

LANES = 128
N_SLAB = D_CONV // LANES
TM = 512
TM_FFN = 1024
N_FFN_TILES = BATCH * SEQ // TM_FFN
SAMPLE_SPLIT = 2
CONV_HALO = 32
POOL_HALO = 16
ROW_BLK = 64
HALF_BLK = ROW_BLK // 2
FF_CHUNKS = ((0, 1024), (1024, 1024), (2048, 768))
ZIP_COLS = 512
ZIP_BLKS = 4
ZIP_LAG = 3
N_CVT = 16
VMEM_LIMIT = 56 * 1024 * 1024

_BF = jnp.bfloat16
_F32 = jnp.float32


def _dot(a, b):
    return jnp.dot(a, b, preferred_element_type=_F32)


def _rmsnorm(x, g):
    return x * lax.rsqrt(jnp.mean(x * x, axis=-1, keepdims=True) + RMS_EPS) * g


def _layernorm(x, g, b):
    mu = jnp.mean(x, axis=-1, keepdims=True)
    xc = x - mu
    var = jnp.mean(xc * xc, axis=-1, keepdims=True)
    return xc * lax.rsqrt(var + LN_EPS) * g + b


def _silu(x):
    return x * jax.nn.sigmoid(x)


def _exact_zero(v):
    u = pltpu.bitcast(v, jnp.uint32)
    u = lax.shift_right_logical(lax.shift_right_logical(u, jnp.uint32(16)), jnp.uint32(16))
    return pltpu.bitcast(u, _F32)


def _layer_spec(tail, layer):
    zeros = (0,) * len(tail)
    return pl.BlockSpec((None,) + tuple(tail), lambda *_: (layer,) + zeros,
                        pipeline_mode=pl.Buffered(1))


def _whole_spec(shape):
    zeros = (0,) * len(shape)
    return pl.BlockSpec(tuple(shape), lambda *_: zeros, pipeline_mode=pl.Buffered(1))


def _cvt_specs(stacked, layer, step_of):
    in_specs, out_specs, out_shape = [], [], []
    for w in stacked:
        _, rows, cols = w.shape
        blk = rows // N_CVT
        assert blk * N_CVT == rows and blk % 16 == 0, w.shape
        at = lambda *g: jnp.minimum(step_of(*g), N_CVT - 1)
        in_specs.append(pl.BlockSpec((None, blk, cols), lambda *g, at=at: (layer, at(*g), 0)))
        out_specs.append(pl.BlockSpec((blk, cols), lambda *g, at=at: (at(*g), 0)))
        out_shape.append(jax.ShapeDtypeStruct((rows, cols), _BF))
    return in_specs, out_specs, out_shape


def _round_weights(src_refs, dst_refs):
    for src, dst in zip(src_refs, dst_refs):
        dst[...] = src[...].astype(_BF)


def _ffn_tile(x, g_ref, wgu_ref, wd_ref, fn_ref, final):
    h = _rmsnorm(x, g_ref[...]).astype(_BF)
    acc = None
    for c0, cw in FF_CHUNKS:
        gate = _dot(h, wgu_ref[:, c0:c0 + cw])
        up = _dot(h, wgu_ref[:, D_FF + c0:D_FF + c0 + cw])
        part = _dot((_silu(gate) * up).astype(_BF), wd_ref[c0:c0 + cw, :])
        acc = part if acc is None else acc + part
    y = x + 0.5 * acc
    if final:
        y = _rmsnorm(y, fn_ref[...])
    return y


def _ffn_body(*refs, n_cvt, final):
    xp_ref, xs_ref, g_ref, wgu_ref, wd_ref, fn_ref = refs[:6]
    cvt_in = refs[6:6 + n_cvt]
    op_ref, os_ref = refs[6 + n_cvt:8 + n_cvt]
    cvt_out = refs[8 + n_cvt:]
    i = pl.program_id(0)

    @pl.when(i < N_FFN_TILES)
    def _():
        op_ref[...] = _ffn_tile(xp_ref[...], g_ref, wgu_ref, wd_ref, fn_ref, final)
        _round_weights(cvt_in, cvt_out)

    @pl.when(i == N_FFN_TILES)
    def _():
        os_ref[...] = _ffn_tile(xs_ref[...], g_ref, wgu_ref, wd_ref, fn_ref, final)


def _ffn(xp, xs, norm, wgu, wd, final_norm, layer, final, to_round, round_layer):
    last_p = N_FFN_TILES - 1
    n_sample = DEC_BATCH * DEC_SEQ
    cvt_in, cvt_out, cvt_shape = _cvt_specs(to_round, round_layer, lambda i: i)
    prompt_tile = pl.BlockSpec((TM_FFN, D_MODEL), lambda i: (jnp.minimum(i, last_p), 0))
    sample_tile = pl.BlockSpec((n_sample, D_MODEL), lambda i: (0, 0))
    out = pl.pallas_call(
        functools.partial(_ffn_body, n_cvt=len(to_round), final=final),
        grid=(N_FFN_TILES + 1,),
        in_specs=[
            prompt_tile,
            pl.BlockSpec((n_sample, D_MODEL), lambda i: (0, 0), pipeline_mode=pl.Buffered(1)),
            _layer_spec((1, D_MODEL), layer),
            _whole_spec((D_MODEL, 2 * D_FF)),
            _whole_spec((D_FF, D_MODEL)),
            pl.BlockSpec((1, D_MODEL), lambda i: (0, 0)),
        ] + cvt_in,
        out_specs=[prompt_tile, sample_tile] + cvt_out,
        out_shape=[jax.ShapeDtypeStruct(xp.shape, _F32), jax.ShapeDtypeStruct(xs.shape, _F32)] + cvt_shape,
        compiler_params=pltpu.CompilerParams(
            dimension_semantics=("arbitrary",), vmem_limit_bytes=VMEM_LIMIT),
        name="ffn",
    )(xp, xs, norm, wgu, wd, final_norm, *to_round)
    return out[0], out[1], tuple(out[2:])


def _pool_linear(m, pw_ref, ps_ref, wpo_ref):
    y = jnp.concatenate(
        [_dot(m[:, g * LANES:(g + 1) * LANES].astype(_BF), pw_ref[g * LANES:(g + 1) * LANES, :])
         for g in range(len(POOL_WINDOWS))], axis=1)
    return _dot((y * ps_ref[...]).astype(_BF), wpo_ref[...])


def _merge_out(x, gate_logits, wo_ref, y_a, y_b, y_c):
    gl = jax.nn.sigmoid(gate_logits)
    merged = (gl[:, 0:D_MODEL] * y_a + gl[:, D_MODEL:2 * D_MODEL] * y_b
              + gl[:, 2 * D_MODEL:3 * D_MODEL] * y_c)
    return x + _dot(merged.astype(_BF), wo_ref[...])


def _mix_prompt_body(*refs, n_cvt):
    (x_ref, g_ref, win_ref, cw_ref, cb_ref, clg_ref, clb_ref, wco_ref,
     pw_ref, ps_ref, wpo_ref, slg_ref, slb_ref, sw_ref, sb_ref, wso_ref, wo_ref) = refs[:17]
    cvt_in = refs[17:17 + n_cvt]
    xo_ref, convo_ref, poolo_ref = refs[17 + n_cvt:20 + n_cvt]
    cvt_out = refs[20 + n_cvt:20 + 2 * n_cvt]
    ah_ref, ph_ref, yc_ref, m_ref, h_ref = refs[20 + 2 * n_cvt:]
    s = pl.program_id(1)

    @pl.when(s == 0)
    def _():
        for g in range(N_SLAB):
            ah_ref[g, 0:CONV_HALO, :] = jnp.zeros((CONV_HALO, LANES), _F32)
            ph_ref[g, 0:POOL_HALO, :] = jnp.zeros((POOL_HALO, LANES), _F32)

    x = x_ref[...]
    h = _rmsnorm(x, g_ref[...]).astype(_BF)

    h_ref[...] = h
    h_tile = h[0:16, 0:LANES]

    def wait_for(dep):
        if dep is not None:
            h_ref[0:16, 0:LANES] = h_tile + jnp.concatenate([dep, dep], axis=0).astype(_BF)

    def glu_slab(g, dep):
        wait_for(dep)
        w = jnp.concatenate([win_ref[:, g * LANES:(g + 1) * LANES],
                             win_ref[:, D_CONV + g * LANES:D_CONV + (g + 1) * LANES]], axis=1)
        a2 = _dot(h_ref[...], w)
        ah_ref[g, CONV_HALO:CONV_HALO + TM, :] = a2[:, 0:LANES] * jax.nn.sigmoid(a2[:, LANES:2 * LANES])
        return a2

    def piece(j, dep):
        wait_for(dep)
        c0 = COL_P + ZIP_COLS * j
        return _dot(h_ref[...], win_ref[:, c0:c0 + ZIP_COLS])

    def conv_group(i, dep):
        for b in range(i * ZIP_BLKS, (i + 1) * ZIP_BLKS):
            g = b // (TM // ROW_BLK)
            base = (b % (TM // ROW_BLK)) * ROW_BLK
            for ph in range(2):
                acc = jnp.concatenate([dep] * (HALF_BLK // 8), axis=0)
                for k in range(CONV_WIDTH):
                    start = base + (CONV_HALO - CONV_BUF) + k + ph
                    acc = acc + cw_ref[g, k:k + 1, :] * ah_ref[g, pl.ds(start, HALF_BLK, stride=2), :]
                yc_ref[g, pl.ds(base + ph, HALF_BLK, stride=2), :] = acc
        return _exact_zero(acc[0:8, :])

    n_piece = (D_IN - COL_P) // ZIP_COLS
    n_grp = N_SLAB * (TM // ROW_BLK) // ZIP_BLKS
    items, conv_dep = [], []
    for j in range(N_SLAB + n_piece):
        dep = conv_dep[j - ZIP_LAG] if 0 <= j - ZIP_LAG < len(conv_dep) else None
        items.append(glu_slab(j, dep) if j < N_SLAB else piece(j - N_SLAB, dep))
        if 1 <= j <= n_grp:
            conv_dep.append(conv_group(j - 1, _exact_zero(items[j][0:8, 0:LANES])))
    assert len(conv_dep) == n_grp
    z_rest = jnp.concatenate(items[N_SLAB:], axis=1)

    yc = jnp.concatenate([yc_ref[g] for g in range(N_SLAB)], axis=1) + cb_ref[...]
    yc = _silu(_layernorm(yc, clg_ref[...], clb_ref[...]))
    y_a = _dot(yc.astype(_BF), wco_ref[...])

    p = z_rest[:, 0:COL_U - COL_P]
    for g in range(N_SLAB):
        ph_ref[g, POOL_HALO:POOL_HALO + TM, :] = p[:, g * LANES:(g + 1) * LANES]

    for i in range(TM // ROW_BLK):
        base = i * ROW_BLK
        for ph in range(2):
            pos = s * TM + base + ph + 2 * lax.broadcasted_iota(jnp.int32, (HALF_BLK, LANES), 0)
            for g, w in enumerate(POOL_WINDOWS):
                cur = ph_ref[g, pl.ds(base + POOL_HALO + ph, HALF_BLK, stride=2), :]
                tot = cur
                for j in range(1, w):
                    tot = tot + ph_ref[g, pl.ds(base + POOL_HALO + ph - j, HALF_BLK, stride=2), :]
                cnt = jnp.minimum(pos + 1, w).astype(_F32)
                m_ref[g, pl.ds(base + ph, HALF_BLK, stride=2), :] = tot / cnt - cur

    m = jnp.concatenate([m_ref[g] for g in range(N_SLAB)], axis=1)
    y_b = _pool_linear(m, pw_ref, ps_ref, wpo_ref)

    u = z_rest[:, COL_U - COL_P:COL_V - COL_P]
    v = z_rest[:, COL_V - COL_P:COL_G - COL_P]
    vn = _layernorm(v, slg_ref[...], slb_ref[...]).astype(_BF)
    n_chunk = TM // CHUNK
    row = lax.broadcasted_iota(jnp.int32, (CHUNK, CHUNK), 0)
    col = lax.broadcasted_iota(jnp.int32, (CHUNK, CHUNK), 1)
    mixed_groups = []
    for g in range(SGU_GROUPS):
        w_causal = jnp.where(row >= col, sw_ref[g], 0.0).astype(_BF)
        vg = vn[:, g * LANES:(g + 1) * LANES]
        rhs = jnp.concatenate([vg[c * CHUNK:(c + 1) * CHUNK, :] for c in range(n_chunk)], axis=1)
        res = _dot(w_causal, rhs)
        mixed_groups.append(
            jnp.concatenate([res[:, c * LANES:(c + 1) * LANES] for c in range(n_chunk)], axis=0))
    bias = jnp.concatenate([sb_ref[...]] * n_chunk, axis=0)
    mixed = jnp.concatenate(mixed_groups, axis=1) + bias
    y_c = _dot((u * mixed).astype(_BF), wso_ref[...])

    xo_ref[...] = _merge_out(x, z_rest[:, COL_G - COL_P:], wo_ref, y_a, y_b, y_c)

    for g in range(N_SLAB):
        convo_ref[:, g * LANES:(g + 1) * LANES] = ah_ref[g, CONV_HALO + TM - CONV_BUF:CONV_HALO + TM, :]
        poolo_ref[:, g * LANES:(g + 1) * LANES] = ph_ref[g, POOL_HALO + TM - POOL_BUF:POOL_HALO + TM, :]
        ah_ref[g, 0:CONV_HALO, :] = ah_ref[g, TM:TM + CONV_HALO, :]
        ph_ref[g, 0:POOL_HALO, :] = ph_ref[g, TM:TM + POOL_HALO, :]

    _round_weights(cvt_in, cvt_out)


def _mixer_weight_specs(layer, sgu_w_tail, sgu_b_tail, conv_w_tail):
    ls = lambda *tail: _layer_spec(tail, layer)
    return [
        ls(1, D_MODEL),
        _whole_spec((D_MODEL, D_IN)),
        ls(*conv_w_tail),
        ls(1, D_CONV), ls(1, D_CONV), ls(1, D_CONV),
        _whole_spec((D_CONV, D_MODEL)),
        _whole_spec((D_POOL, LANES)),
        ls(1, D_POOL),
        _whole_spec((D_POOL, D_MODEL)),
        ls(1, D_SGU), ls(1, D_SGU),
        ls(*sgu_w_tail),
        ls(*sgu_b_tail),
        _whole_spec((D_SGU, D_MODEL)),
        _whole_spec((D_MODEL, D_MODEL)),
    ]


def _mix_prompt(x, layer, prm, to_round):
    n_s = SEQ // TM
    tok = pl.BlockSpec((TM, D_MODEL), lambda b, s: (b * n_s + s, 0))
    cvt_in, cvt_out, cvt_shape = _cvt_specs(to_round, layer, lambda b, s: b * n_s + s)
    in_specs = [tok] + _mixer_weight_specs(
        layer, (SGU_GROUPS, CHUNK, CHUNK), (CHUNK, D_SGU), (N_SLAB, CONV_WIDTH, LANES)) + cvt_in
    out_specs = [
        tok,
        pl.BlockSpec((None, CONV_BUF, D_CONV), lambda b, s: (b, 0, 0)),
        pl.BlockSpec((None, POOL_BUF, D_POOL), lambda b, s: (b, 0, 0)),
    ] + cvt_out
    out_shape = [
        jax.ShapeDtypeStruct((BATCH * SEQ, D_MODEL), _F32),
        jax.ShapeDtypeStruct((BATCH, CONV_BUF, D_CONV), _F32),
        jax.ShapeDtypeStruct((BATCH, POOL_BUF, D_POOL), _F32),
    ] + cvt_shape
    scratch = [
        pltpu.VMEM((N_SLAB, CONV_HALO + TM, LANES), _F32),
        pltpu.VMEM((N_SLAB, POOL_HALO + TM, LANES), _F32),
        pltpu.VMEM((N_SLAB, TM, LANES), _F32),
        pltpu.VMEM((N_SLAB, TM, LANES), _F32),
        pltpu.VMEM((TM, D_MODEL), _BF),
    ]
    out = pl.pallas_call(
        functools.partial(_mix_prompt_body, n_cvt=len(to_round)),
        grid=(BATCH, n_s),
        in_specs=in_specs, out_specs=out_specs, out_shape=out_shape,
        scratch_shapes=scratch,
        compiler_params=pltpu.CompilerParams(
            dimension_semantics=("arbitrary", "arbitrary"), vmem_limit_bytes=VMEM_LIMIT),
        name="mix_prompt",
    )(x, *prm, *to_round)
    return out[0], out[1], out[2], tuple(out[3:])


def _mix_sample_body(x_ref, sc_ref, sp_ref, g_ref, win_ref, cw_ref, cb_ref, clg_ref, clb_ref, wco_ref,
                     pw_ref, ps_ref, wpo_ref, slg_ref, slb_ref, sw4_ref, sb4_ref, wso_ref, wo_ref,
                     xo_ref, convo_ref, poolo_ref, vn_ref):
    nb = DEC_BATCH // SAMPLE_SPLIT
    x = x_ref[...].reshape(DEC_SEQ * nb, D_MODEL)
    h = _rmsnorm(x, g_ref[...]).astype(_BF)

    a2 = _dot(h, win_ref[:, 0:COL_P])
    a = a2[:, 0:D_CONV] * jax.nn.sigmoid(a2[:, D_CONV:2 * D_CONV])
    a_t = [a[t * nb:(t + 1) * nb, :] for t in range(DEC_SEQ)]
    padded = lambda j: sc_ref[j] if j < CONV_BUF else a_t[j - CONV_BUF]
    ys = []
    for t in range(DEC_SEQ):
        acc = jnp.zeros((nb, D_CONV), _F32) + cb_ref[...]
        for k in range(CONV_WIDTH):
            acc = acc + cw_ref[k:k + 1, :] * padded(t + k)
        ys.append(acc)
    for j in range(CONV_BUF):
        convo_ref[j] = padded(j + DEC_SEQ)
    yc = _silu(_layernorm(jnp.concatenate(ys, axis=0), clg_ref[...], clb_ref[...]))
    y_a = _dot(yc.astype(_BF), wco_ref[...])

    p = _dot(h, win_ref[:, COL_P:COL_U])
    p_t = [p[t * nb:(t + 1) * nb, :] for t in range(DEC_SEQ)]
    ppad = lambda j: sp_ref[j] if j < POOL_BUF else p_t[j - POOL_BUF]
    ms = []
    for t in range(DEC_SEQ):
        parts = []
        for g, w in enumerate(POOL_WINDOWS):
            sl = slice(g * LANES, (g + 1) * LANES)
            tot = p_t[t][:, sl]
            for j in range(1, w):
                tot = tot + ppad(POOL_BUF + t - j)[:, sl]
            cnt = float(min(PAST_LEN + t + 1, w))
            parts.append(tot / cnt - p_t[t][:, sl])
        ms.append(jnp.concatenate(parts, axis=1))
    for j in range(POOL_BUF):
        poolo_ref[j] = ppad(j + DEC_SEQ)
    y_b = _pool_linear(jnp.concatenate(ms, axis=0), pw_ref, ps_ref, wpo_ref)

    u = _dot(h, win_ref[:, COL_U:COL_V])
    v = _dot(h, win_ref[:, COL_V:COL_G])
    vn = _layernorm(v, slg_ref[...], slb_ref[...])
    vn_ref[...] = vn.reshape(DEC_SEQ, nb, D_SGU)
    vn_t = [vn[t * nb:(t + 1) * nb, :] for t in range(DEC_SEQ)]
    mixed = []
    for t in range(DEC_SEQ):
        acc = jnp.zeros((nb, D_SGU), _F32) + sb4_ref[t:t + 1, :]
        for src in range(t + 1):
            acc = acc + sw4_ref[t, src:src + 1, :] * vn_t[src]
        mixed.append(acc)
    y_c = _dot((u * jnp.concatenate(mixed, axis=0)).astype(_BF), wso_ref[...])

    y = _merge_out(x, _dot(h, win_ref[:, COL_G:D_IN]), wo_ref, y_a, y_b, y_c)
    xo_ref[...] = y.reshape(DEC_SEQ, nb, D_MODEL)


def _mix_sample(x, sc_t, sp_t, layer, prm):
    nb = DEC_BATCH // SAMPLE_SPLIT
    seq_blk = lambda rows, width: pl.BlockSpec((rows, nb, width), lambda i: (0, i, 0))
    state_blk = lambda rows, width: pl.BlockSpec((None, rows, nb, width), lambda i: (layer, 0, i, 0))
    in_specs = [
        seq_blk(DEC_SEQ, D_MODEL),
        state_blk(CONV_BUF, D_CONV),
        state_blk(POOL_BUF, D_POOL),
    ] + _mixer_weight_specs(layer, (DEC_SEQ, DEC_SEQ, D_SGU), (DEC_SEQ, D_SGU), (CONV_WIDTH, D_CONV))
    out_specs = [
        seq_blk(DEC_SEQ, D_MODEL),
        seq_blk(CONV_BUF, D_CONV),
        seq_blk(POOL_BUF, D_POOL),
        seq_blk(DEC_SEQ, D_SGU),
    ]
    out_shape = [
        jax.ShapeDtypeStruct((DEC_SEQ, DEC_BATCH, D_MODEL), _F32),
        jax.ShapeDtypeStruct((CONV_BUF, DEC_BATCH, D_CONV), _F32),
        jax.ShapeDtypeStruct((POOL_BUF, DEC_BATCH, D_POOL), _F32),
        jax.ShapeDtypeStruct((DEC_SEQ, DEC_BATCH, D_SGU), _F32),
    ]
    return pl.pallas_call(
        _mix_sample_body,
        grid=(SAMPLE_SPLIT,),
        in_specs=in_specs, out_specs=out_specs, out_shape=out_shape,
        compiler_params=pltpu.CompilerParams(
            dimension_semantics=("arbitrary",), vmem_limit_bytes=VMEM_LIMIT),
        name="mix_sample",
    )(x, sc_t, sp_t, *prm)


def kernel(x_prompt, x_sample, state_conv, state_pool, ffn1_norm, ffn1_w_gate_up, ffn1_w_down, mix_norm, w_in, conv_dw_w, conv_dw_b, conv_ln_g, conv_ln_b, w_conv_out, pool_w, pool_scale, w_pool_out, sgu_ln_g, sgu_ln_b, sgu_w, sgu_b, w_sgu_out, w_o, ffn2_norm, ffn2_w_gate_up, ffn2_w_down, final_norm):
    row = lambda p: p.reshape(DEPTH, 1, p.shape[-1])
    fnorm = final_norm.reshape(1, D_MODEL)
    ffn1_f32 = (ffn1_w_gate_up, ffn1_w_down)
    ffn2_f32 = (ffn2_w_gate_up, ffn2_w_down)
    mix_f32 = (w_in, w_conv_out, pool_w.reshape(DEPTH, D_POOL, LANES), w_pool_out, w_sgu_out, w_o)

    def mixer_params(mix_bf, conv_w, sgu_wt, sgu_bias):
        win_bf, wco_bf, pw_bf, wpo_bf, wso_bf, wo_bf = mix_bf
        return (row(mix_norm), win_bf, conv_w, row(conv_dw_b), row(conv_ln_g), row(conv_ln_b), wco_bf,
                pw_bf, row(pool_scale), wpo_bf, row(sgu_ln_g), row(sgu_ln_b), sgu_wt, sgu_bias, wso_bf, wo_bf)

    cw_slab = conv_dw_w.reshape(DEPTH, CONV_WIDTH, N_SLAB, LANES).transpose(0, 2, 1, 3)
    sb_full = jnp.repeat(jnp.swapaxes(sgu_b, 1, 2), LANES, axis=2)
    sw4 = jnp.tril(sgu_w[:, :, :DEC_SEQ, :DEC_SEQ])
    sw4 = jnp.repeat(jnp.transpose(sw4, (0, 2, 3, 1)), LANES, axis=3)
    sb4 = jnp.repeat(jnp.swapaxes(sgu_b[:, :, :DEC_SEQ], 1, 2), LANES, axis=2)

    xp = x_prompt.reshape(BATCH * SEQ, D_MODEL)
    xs = jnp.swapaxes(x_sample, 0, 1).reshape(DEC_SEQ * DEC_BATCH, D_MODEL)
    sc_t = jnp.swapaxes(state_conv, 1, 2)
    sp_t = jnp.swapaxes(state_pool, 1, 2)

    ffn1_bf = (ffn1_w_gate_up[0].astype(_BF), ffn1_w_down[0].astype(_BF))
    conv_p, pool_p, conv_s, pool_s, vn_s = [], [], [], [], []
    for layer in range(DEPTH):
        last = layer == DEPTH - 1
        xp, xs, mix_bf = _ffn(xp, xs, row(ffn1_norm), *ffn1_bf, fnorm, layer, False, mix_f32, layer)
        xp, cp, pp, ffn2_bf = _mix_prompt(
            xp, layer, mixer_params(mix_bf, cw_slab, sgu_w, sb_full), ffn2_f32)
        xs, cs, ps, vn = _mix_sample(xs.reshape(DEC_SEQ, DEC_BATCH, D_MODEL), sc_t, sp_t, layer,
                                     mixer_params(mix_bf, conv_dw_w, sw4, sb4))
        xs = xs.reshape(DEC_SEQ * DEC_BATCH, D_MODEL)
        xp, xs, ffn1_bf = _ffn(xp, xs, row(ffn2_norm), *ffn2_bf, fnorm, layer, last,
                               () if last else ffn1_f32, layer + 1)
        conv_p.append(cp)
        pool_p.append(pp)
        conv_s.append(jnp.swapaxes(cs, 0, 1))
        pool_s.append(jnp.swapaxes(ps, 0, 1))
        vn_s.append(jnp.swapaxes(vn, 0, 1))

    y_prompt = xp.reshape(BATCH, SEQ, D_MODEL)
    y_sample = jnp.swapaxes(xs.reshape(DEC_SEQ, DEC_BATCH, D_MODEL), 0, 1)
    return (y_prompt, y_sample, jnp.stack(conv_p), jnp.stack(conv_s),
            jnp.stack(pool_p), jnp.stack(pool_s), jnp.stack(vn_s))
```

```python
import functools

import jax
import jax.numpy as jnp
from jax import lax
from jax.experimental import pallas as pl
from jax.experimental.pallas import tpu as pltpu

D_MODEL = 1024
BATCH = 8
SEQ = 2048
DEPTH = 4
DEC_BATCH = 128
DEC_SEQ = 4
PAST_LEN = 16384
D_CONV = 512
CONV_WIDTH = 31
CONV_BUF = CONV_WIDTH - 1
D_POOL = 512
POOL_WINDOWS = (2, 4, 8, 16)
POOL_BUF = 15
D_SGU = 512
SGU_GROUPS = 4
CHUNK = 128
D_FF = 2816
D_IN = 5632
COL_P = 2 * D_CONV
COL_U = COL_P + D_POOL
COL_V = COL_U + D_SGU
COL_G = COL_V + D_SGU
RMS_EPS = 1e-6
LN_EPS = 1e-5

LANES = 128
N_SLAB = D_CONV // LANES
TM = 512
TM_FFN = 1024
N_FFN_TILES = BATCH * SEQ // TM_FFN
SAMPLE_SPLIT = 2
CONV_HALO = 32
POOL_HALO = 16
ROW_BLK = 64
HALF_BLK = ROW_BLK // 2
FF_CHUNKS = ((0, 1024), (1024, 1024), (2048, 768))
ZIP_COLS = 512
ZIP_BLKS = 4
ZIP_LAG = 3
N_CVT = 16
VMEM_LIMIT = 56 * 1024 * 1024

_BF = jnp.bfloat16
_F32 = jnp.float32


def _dot(a, b):
    return jnp.dot(a, b, preferred_element_type=_F32)


def _rmsnorm(x, g):
    return x * lax.rsqrt(jnp.mean(x * x, axis=-1, keepdims=True) + RMS_EPS) * g


def _layernorm(x, g, b):
    mu = jnp.mean(x, axis=-1, keepdims=True)
    xc = x - mu
    var = jnp.mean(xc * xc, axis=-1, keepdims=True)
    return xc * lax.rsqrt(var + LN_EPS) * g + b


def _silu(x):
    return x * jax.nn.sigmoid(x)


def _exact_zero(v):
    u = pltpu.bitcast(v, jnp.uint32)
    u = lax.shift_right_logical(lax.shift_right_logical(u, jnp.uint32(16)), jnp.uint32(16))
    return pltpu.bitcast(u, _F32)


def _layer_spec(tail, layer):
    zeros = (0,) * len(tail)
    return pl.BlockSpec((None,) + tuple(tail), lambda *_: (layer,) + zeros,
                        pipeline_mode=pl.Buffered(1))


def _whole_spec(shape):
    zeros = (0,) * len(shape)
    return pl.BlockSpec(tuple(shape), lambda *_: zeros, pipeline_mode=pl.Buffered(1))


def _cvt_specs(stacked, layers, step_of):
    in_specs, out_specs, out_shape = [], [], []
    for w, layer in zip(stacked, layers):
        _, rows, cols = w.shape
        blk = rows // N_CVT
        assert blk * N_CVT == rows and blk % 16 == 0, w.shape
        at = lambda *g: jnp.minimum(step_of(*g), N_CVT - 1)
        in_specs.append(pl.BlockSpec((None, blk, cols), lambda *g, at=at, layer=layer: (layer, at(*g), 0)))
        out_specs.append(pl.BlockSpec((blk, cols), lambda *g, at=at: (at(*g), 0)))
        out_shape.append(jax.ShapeDtypeStruct((rows, cols), _BF))
    return in_specs, out_specs, out_shape


def _round_weights(src_refs, dst_refs):
    for src, dst in zip(src_refs, dst_refs):
        dst[...] = src[...].astype(_BF)


def _ffn_tile(x, g_ref, wgu_ref, wd_ref, fn_ref, final):
    h = _rmsnorm(x, g_ref[...]).astype(_BF)
    acc = None
    for c0, cw in FF_CHUNKS:
        gate = _dot(h, wgu_ref[:, c0:c0 + cw])
        up = _dot(h, wgu_ref[:, D_FF + c0:D_FF + c0 + cw])
        part = _dot((_silu(gate) * up).astype(_BF), wd_ref[c0:c0 + cw, :])
        acc = part if acc is None else acc + part
    y = x + 0.5 * acc
    if final:
        y = _rmsnorm(y, fn_ref[...])
    return y


def _ffn_body(*refs, n_cvt, final):
    xp_ref, xs_ref, g_ref, wgu_ref, wd_ref, fn_ref = refs[:6]
    cvt_in = refs[6:6 + n_cvt]
    op_ref, os_ref = refs[6 + n_cvt:8 + n_cvt]
    cvt_out = refs[8 + n_cvt:]
    i = pl.program_id(0)

    @pl.when(i < N_FFN_TILES)
    def _():
        op_ref[...] = _ffn_tile(xp_ref[...], g_ref, wgu_ref, wd_ref, fn_ref, final)
        _round_weights(cvt_in, cvt_out)

    @pl.when(i == N_FFN_TILES)
    def _():
        os_ref[...] = _ffn_tile(xs_ref[...], g_ref, wgu_ref, wd_ref, fn_ref, final)


def _ffn(xp, xs, norm, wgu, wd, final_norm, layer, final, to_round, round_layer):
    last_p = N_FFN_TILES - 1
    n_sample = DEC_BATCH * DEC_SEQ
    cvt_in, cvt_out, cvt_shape = _cvt_specs(to_round, [round_layer] * len(to_round), lambda i: i)
    prompt_tile = pl.BlockSpec((TM_FFN, D_MODEL), lambda i: (jnp.minimum(i, last_p), 0))
    sample_tile = pl.BlockSpec((n_sample, D_MODEL), lambda i: (0, 0))
    out = pl.pallas_call(
        functools.partial(_ffn_body, n_cvt=len(to_round), final=final),
        grid=(N_FFN_TILES + 1,),
        in_specs=[
            prompt_tile,
            pl.BlockSpec((n_sample, D_MODEL), lambda i: (0, 0), pipeline_mode=pl.Buffered(1)),
            _layer_spec((1, D_MODEL), layer),
            _whole_spec((D_MODEL, 2 * D_FF)),
            _whole_spec((D_FF, D_MODEL)),
            pl.BlockSpec((1, D_MODEL), lambda i: (0, 0)),
        ] + cvt_in,
        out_specs=[prompt_tile, sample_tile] + cvt_out,
        out_shape=[jax.ShapeDtypeStruct(xp.shape, _F32), jax.ShapeDtypeStruct(xs.shape, _F32)] + cvt_shape,
        compiler_params=pltpu.CompilerParams(
            dimension_semantics=("arbitrary",), vmem_limit_bytes=VMEM_LIMIT),
        name="ffn",
    )(xp, xs, norm, wgu, wd, final_norm, *to_round)
    return out[0], out[1], tuple(out[2:])


def _pool_linear(m, pw_ref, ps_ref, wpo_ref):
    y = jnp.concatenate(
        [_dot(m[:, g * LANES:(g + 1) * LANES].astype(_BF), pw_ref[g * LANES:(g + 1) * LANES, :])
         for g in range(len(POOL_WINDOWS))], axis=1)
    return _dot((y * ps_ref[...]).astype(_BF), wpo_ref[...])


def _merge_out(x, gate_logits, wo_ref, y_a, y_b, y_c):
    gl = jax.nn.sigmoid(gate_logits)
    merged = (gl[:, 0:D_MODEL] * y_a + gl[:, D_MODEL:2 * D_MODEL] * y_b
              + gl[:, 2 * D_MODEL:3 * D_MODEL] * y_c)
    return x + _dot(merged.astype(_BF), wo_ref[...])


def _mix_prompt_body(*refs, n_cvt):
    (x_ref, g_ref, win_ref, cw_ref, cb_ref, clg_ref, clb_ref, wco_ref,
     pw_ref, ps_ref, wpo_ref, slg_ref, slb_ref, sw_ref, sb_ref, wso_ref, wo_ref) = refs[:17]
    cvt_in = refs[17:17 + n_cvt]
    xo_ref, convo_ref, poolo_ref = refs[17 + n_cvt:20 + n_cvt]
    cvt_out = refs[20 + n_cvt:20 + 2 * n_cvt]
    ah_ref, ph_ref, yc_ref, m_ref, h_ref = refs[20 + 2 * n_cvt:]
    s = pl.program_id(1)

    @pl.when(s == 0)
    def _():
        for g in range(N_SLAB):
            ah_ref[g, 0:CONV_HALO, :] = jnp.zeros((CONV_HALO, LANES), _F32)
            ph_ref[g, 0:POOL_HALO, :] = jnp.zeros((POOL_HALO, LANES), _F32)

    x = x_ref[...]
    h = _rmsnorm(x, g_ref[...]).astype(_BF)

    h_ref[...] = h
    h_tile = h[0:16, 0:LANES]

    def wait_for(dep):
        if dep is not None:
            h_ref[0:16, 0:LANES] = h_tile + jnp.concatenate([dep, dep], axis=0).astype(_BF)

    def glu_slab(g, dep):
        wait_for(dep)
        w = jnp.concatenate([win_ref[:, g * LANES:(g + 1) * LANES],
                             win_ref[:, D_CONV + g * LANES:D_CONV + (g + 1) * LANES]], axis=1)
        a2 = _dot(h_ref[...], w)
        ah_ref[g, CONV_HALO:CONV_HALO + TM, :] = a2[:, 0:LANES] * jax.nn.sigmoid(a2[:, LANES:2 * LANES])
        return a2

    def piece(j, dep):
        wait_for(dep)
        c0 = COL_P + ZIP_COLS * j
        return _dot(h_ref[...], win_ref[:, c0:c0 + ZIP_COLS])

    def conv_group(i, dep):
        for b in range(i * ZIP_BLKS, (i + 1) * ZIP_BLKS):
            g = b // (TM // ROW_BLK)
            base = (b % (TM // ROW_BLK)) * ROW_BLK
            for ph in range(2):
                acc = jnp.concatenate([dep] * (HALF_BLK // 8), axis=0)
                for k in range(CONV_WIDTH):
                    start = base + (CONV_HALO - CONV_BUF) + k + ph
                    acc = acc + cw_ref[g, k:k + 1, :] * ah_ref[g, pl.ds(start, HALF_BLK, stride=2), :]
                yc_ref[g, pl.ds(base + ph, HALF_BLK, stride=2), :] = acc
        return _exact_zero(acc[0:8, :])

    n_piece = (D_IN - COL_P) // ZIP_COLS
    n_grp = N_SLAB * (TM // ROW_BLK) // ZIP_BLKS
    items, conv_dep = [], []
    for j in range(N_SLAB + n_piece):
        dep = conv_dep[j - ZIP_LAG] if 0 <= j - ZIP_LAG < len(conv_dep) else None
        items.append(glu_slab(j, dep) if j < N_SLAB else piece(j - N_SLAB, dep))
        if 1 <= j <= n_grp:
            conv_dep.append(conv_group(j - 1, _exact_zero(items[j][0:8, 0:LANES])))
    assert len(conv_dep) == n_grp
    z_rest = jnp.concatenate(items[N_SLAB:], axis=1)

    yc = jnp.concatenate([yc_ref[g] for g in range(N_SLAB)], axis=1) + cb_ref[...]
    yc = _silu(_layernorm(yc, clg_ref[...], clb_ref[...]))
    y_a = _dot(yc.astype(_BF), wco_ref[...])

    p = z_rest[:, 0:COL_U - COL_P]
    for g in range(N_SLAB):
        ph_ref[g, POOL_HALO:POOL_HALO + TM, :] = p[:, g * LANES:(g + 1) * LANES]

    for i in range(TM // ROW_BLK):
        base = i * ROW_BLK
        for ph in range(2):
            pos = s * TM + base + ph + 2 * lax.broadcasted_iota(jnp.int32, (HALF_BLK, LANES), 0)
            for g, w in enumerate(POOL_WINDOWS):
                cur = ph_ref[g, pl.ds(base + POOL_HALO + ph, HALF_BLK, stride=2), :]
                tot = cur
                for j in range(1, w):
                    tot = tot + ph_ref[g, pl.ds(base + POOL_HALO + ph - j, HALF_BLK, stride=2), :]
                cnt = jnp.minimum(pos + 1, w).astype(_F32)
                m_ref[g, pl.ds(base + ph, HALF_BLK, stride=2), :] = tot / cnt - cur

    m = jnp.concatenate([m_ref[g] for g in range(N_SLAB)], axis=1)
    y_b = _pool_linear(m, pw_ref, ps_ref, wpo_ref)

    u = z_rest[:, COL_U - COL_P:COL_V - COL_P]
    v = z_rest[:, COL_V - COL_P:COL_G - COL_P]
    vn = _layernorm(v, slg_ref[...], slb_ref[...]).astype(_BF)
    n_chunk = TM // CHUNK
    row = lax.broadcasted_iota(jnp.int32, (CHUNK, CHUNK), 0)
    col = lax.broadcasted_iota(jnp.int32, (CHUNK, CHUNK), 1)
    mixed_groups = []
    for g in range(SGU_GROUPS):
        w_causal = jnp.where(row >= col, sw_ref[g], 0.0).astype(_BF)
        vg = vn[:, g * LANES:(g + 1) * LANES]
        rhs = jnp.concatenate([vg[c * CHUNK:(c + 1) * CHUNK, :] for c in range(n_chunk)], axis=1)
        res = _dot(w_causal, rhs)
        mixed_groups.append(
            jnp.concatenate([res[:, c * LANES:(c + 1) * LANES] for c in range(n_chunk)], axis=0))
    bias = jnp.concatenate([sb_ref[...]] * n_chunk, axis=0)
    mixed = jnp.concatenate(mixed_groups, axis=1) + bias
    y_c = _dot((u * mixed).astype(_BF), wso_ref[...])

    xo_ref[...] = _merge_out(x, z_rest[:, COL_G - COL_P:], wo_ref, y_a, y_b, y_c)

    for g in range(N_SLAB):
        convo_ref[:, g * LANES:(g + 1) * LANES] = ah_ref[g, CONV_HALO + TM - CONV_BUF:CONV_HALO + TM, :]
        poolo_ref[:, g * LANES:(g + 1) * LANES] = ph_ref[g, POOL_HALO + TM - POOL_BUF:POOL_HALO + TM, :]
        ah_ref[g, 0:CONV_HALO, :] = ah_ref[g, TM:TM + CONV_HALO, :]
        ph_ref[g, 0:POOL_HALO, :] = ph_ref[g, TM:TM + POOL_HALO, :]

    _round_weights(cvt_in, cvt_out)


def _mixer_weight_specs(layer, sgu_w_tail, sgu_b_tail, conv_w_tail):
    ls = lambda *tail: _layer_spec(tail, layer)
    return [
        ls(1, D_MODEL),
        _whole_spec((D_MODEL, D_IN)),
        ls(*conv_w_tail),
        ls(1, D_CONV), ls(1, D_CONV), ls(1, D_CONV),
        _whole_spec((D_CONV, D_MODEL)),
        _whole_spec((D_POOL, LANES)),
        ls(1, D_POOL),
        _whole_spec((D_POOL, D_MODEL)),
        ls(1, D_SGU), ls(1, D_SGU),
        ls(*sgu_w_tail),
        ls(*sgu_b_tail),
        _whole_spec((D_SGU, D_MODEL)),
        _whole_spec((D_MODEL, D_MODEL)),
    ]


def _mix_prompt(x, layer, prm, to_round, round_layers):
    n_s = SEQ // TM
    tok = pl.BlockSpec((TM, D_MODEL), lambda b, s: (b * n_s + s, 0))
    cvt_in, cvt_out, cvt_shape = _cvt_specs(to_round, round_layers, lambda b, s: b * n_s + s)
    in_specs = [tok] + _mixer_weight_specs(
        layer, (SGU_GROUPS, CHUNK, CHUNK), (CHUNK, D_SGU), (N_SLAB, CONV_WIDTH, LANES)) + cvt_in
    out_specs = [
        tok,
        pl.BlockSpec((None, CONV_BUF, D_CONV), lambda b, s: (b, 0, 0)),
        pl.BlockSpec((None, POOL_BUF, D_POOL), lambda b, s: (b, 0, 0)),
    ] + cvt_out
    out_shape = [
        jax.ShapeDtypeStruct((BATCH * SEQ, D_MODEL), _F32),
        jax.ShapeDtypeStruct((BATCH, CONV_BUF, D_CONV), _F32),
        jax.ShapeDtypeStruct((BATCH, POOL_BUF, D_POOL), _F32),
    ] + cvt_shape
    scratch = [
        pltpu.VMEM((N_SLAB, CONV_HALO + TM, LANES), _F32),
        pltpu.VMEM((N_SLAB, POOL_HALO + TM, LANES), _F32),
        pltpu.VMEM((N_SLAB, TM, LANES), _F32),
        pltpu.VMEM((N_SLAB, TM, LANES), _F32),
        pltpu.VMEM((TM, D_MODEL), _BF),
    ]
    out = pl.pallas_call(
        functools.partial(_mix_prompt_body, n_cvt=len(to_round)),
        grid=(BATCH, n_s),
        in_specs=in_specs, out_specs=out_specs, out_shape=out_shape,
        scratch_shapes=scratch,
        compiler_params=pltpu.CompilerParams(
            dimension_semantics=("arbitrary", "arbitrary"), vmem_limit_bytes=VMEM_LIMIT),
        name="mix_prompt",
    )(x, *prm, *to_round)
    return out[0], out[1], out[2], tuple(out[3:])


def _mix_sample_body(x_ref, sc_ref, sp_ref, g_ref, win_ref, cw_ref, cb_ref, clg_ref, clb_ref, wco_ref,
                     pw_ref, ps_ref, wpo_ref, slg_ref, slb_ref, sw4_ref, sb4_ref, wso_ref, wo_ref,
                     xo_ref, convo_ref, poolo_ref, vn_ref):
    nb = DEC_BATCH // SAMPLE_SPLIT
    x = x_ref[...].reshape(DEC_SEQ * nb, D_MODEL)
    h = _rmsnorm(x, g_ref[...]).astype(_BF)

    a2 = _dot(h, win_ref[:, 0:COL_P])
    a = a2[:, 0:D_CONV] * jax.nn.sigmoid(a2[:, D_CONV:2 * D_CONV])
    a_t = [a[t * nb:(t + 1) * nb, :] for t in range(DEC_SEQ)]
    padded = lambda j: sc_ref[j] if j < CONV_BUF else a_t[j - CONV_BUF]
    ys = []
    for t in range(DEC_SEQ):
        acc = jnp.zeros((nb, D_CONV), _F32) + cb_ref[...]
        for k in range(CONV_WIDTH):
            acc = acc + cw_ref[k:k + 1, :] * padded(t + k)
        ys.append(acc)
    for j in range(CONV_BUF):
        convo_ref[j] = padded(j + DEC_SEQ)
    yc = _silu(_layernorm(jnp.concatenate(ys, axis=0), clg_ref[...], clb_ref[...]))
    y_a = _dot(yc.astype(_BF), wco_ref[...])

    p = _dot(h, win_ref[:, COL_P:COL_U])
    p_t = [p[t * nb:(t + 1) * nb, :] for t in range(DEC_SEQ)]
    ppad = lambda j: sp_ref[j] if j < POOL_BUF else p_t[j - POOL_BUF]
    ms = []
    for t in range(DEC_SEQ):
        parts = []
        for g, w in enumerate(POOL_WINDOWS):
            sl = slice(g * LANES, (g + 1) * LANES)
            tot = p_t[t][:, sl]
            for j in range(1, w):
                tot = tot + ppad(POOL_BUF + t - j)[:, sl]
            cnt = float(min(PAST_LEN + t + 1, w))
            parts.append(tot / cnt - p_t[t][:, sl])
        ms.append(jnp.concatenate(parts, axis=1))
    for j in range(POOL_BUF):
        poolo_ref[j] = ppad(j + DEC_SEQ)
    y_b = _pool_linear(jnp.concatenate(ms, axis=0), pw_ref, ps_ref, wpo_ref)

    u = _dot(h, win_ref[:, COL_U:COL_V])
    v = _dot(h, win_ref[:, COL_V:COL_G])
    vn = _layernorm(v, slg_ref[...], slb_ref[...])
    vn_ref[...] = vn.reshape(DEC_SEQ, nb, D_SGU)
    vn_t = [vn[t * nb:(t + 1) * nb, :] for t in range(DEC_SEQ)]
    mixed = []
    for t in range(DEC_SEQ):
        acc = jnp.zeros((nb, D_SGU), _F32) + sb4_ref[t:t + 1, :]
        for src in range(t + 1):
            acc = acc + sw4_ref[t, src:src + 1, :] * vn_t[src]
        mixed.append(acc)
    y_c = _dot((u * jnp.concatenate(mixed, axis=0)).astype(_BF), wso_ref[...])

    y = _merge_out(x, _dot(h, win_ref[:, COL_G:D_IN]), wo_ref, y_a, y_b, y_c)
    xo_ref[...] = y.reshape(DEC_SEQ, nb, D_MODEL)


def _mix_sample(x, sc_t, sp_t, layer, prm):
    nb = DEC_BATCH // SAMPLE_SPLIT
    seq_blk = lambda rows, width: pl.BlockSpec((rows, nb, width), lambda i: (0, i, 0))
    state_blk = lambda rows, width: pl.BlockSpec((None, rows, nb, width), lambda i: (layer, 0, i, 0))
    in_specs = [
        seq_blk(DEC_SEQ, D_MODEL),
        state_blk(CONV_BUF, D_CONV),
        state_blk(POOL_BUF, D_POOL),
    ] + _mixer_weight_specs(layer, (DEC_SEQ, DEC_SEQ, D_SGU), (DEC_SEQ, D_SGU), (CONV_WIDTH, D_CONV))
    out_specs = [
        seq_blk(DEC_SEQ, D_MODEL),
        seq_blk(CONV_BUF, D_CONV),
        seq_blk(POOL_BUF, D_POOL),
        seq_blk(DEC_SEQ, D_SGU),
    ]
    out_shape = [
        jax.ShapeDtypeStruct((DEC_SEQ, DEC_BATCH, D_MODEL), _F32),
        jax.ShapeDtypeStruct((CONV_BUF, DEC_BATCH, D_CONV), _F32),
        jax.ShapeDtypeStruct((POOL_BUF, DEC_BATCH, D_POOL), _F32),
        jax.ShapeDtypeStruct((DEC_SEQ, DEC_BATCH, D_SGU), _F32),
    ]
    return pl.pallas_call(
        _mix_sample_body,
        grid=(SAMPLE_SPLIT,),
        in_specs=in_specs, out_specs=out_specs, out_shape=out_shape,
        compiler_params=pltpu.CompilerParams(
            dimension_semantics=("arbitrary",), vmem_limit_bytes=VMEM_LIMIT),
        name="mix_sample",
    )(x, sc_t, sp_t, *prm)


def kernel(x_prompt, x_sample, state_conv, state_pool, ffn1_norm, ffn1_w_gate_up, ffn1_w_down, mix_norm, w_in, conv_dw_w, conv_dw_b, conv_ln_g, conv_ln_b, w_conv_out, pool_w, pool_scale, w_pool_out, sgu_ln_g, sgu_ln_b, sgu_w, sgu_b, w_sgu_out, w_o, ffn2_norm, ffn2_w_gate_up, ffn2_w_down, final_norm):
    row = lambda p: p.reshape(DEPTH, 1, p.shape[-1])
    fnorm = final_norm.reshape(1, D_MODEL)
    ffn1_f32 = (ffn1_w_gate_up, ffn1_w_down)
    ffn2_f32 = (ffn2_w_gate_up, ffn2_w_down)
    mix_f32 = (w_in, w_conv_out, pool_w.reshape(DEPTH, D_POOL, LANES), w_pool_out, w_sgu_out, w_o)

    def mixer_params(mix_bf, conv_w, sgu_wt, sgu_bias):
        win_bf, wco_bf, pw_bf, wpo_bf, wso_bf, wo_bf = mix_bf
        return (row(mix_norm), win_bf, conv_w, row(conv_dw_b), row(conv_ln_g), row(conv_ln_b), wco_bf,
                pw_bf, row(pool_scale), wpo_bf, row(sgu_ln_g), row(sgu_ln_b), sgu_wt, sgu_bias, wso_bf, wo_bf)

    cw_slab = conv_dw_w.reshape(DEPTH, CONV_WIDTH, N_SLAB, LANES).transpose(0, 2, 1, 3)
    sb_full = jnp.repeat(jnp.swapaxes(sgu_b, 1, 2), LANES, axis=2)
    sw4 = jnp.tril(sgu_w[:, :, :DEC_SEQ, :DEC_SEQ])
    sw4 = jnp.repeat(jnp.transpose(sw4, (0, 2, 3, 1)), LANES, axis=3)
    sb4 = jnp.repeat(jnp.swapaxes(sgu_b[:, :, :DEC_SEQ], 1, 2), LANES, axis=2)

    xp = x_prompt.reshape(BATCH * SEQ, D_MODEL)
    xs = jnp.swapaxes(x_sample, 0, 1).reshape(DEC_SEQ * DEC_BATCH, D_MODEL)
    sc_t = jnp.swapaxes(state_conv, 1, 2)
    sp_t = jnp.swapaxes(state_pool, 1, 2)

    ffn1_bf = (ffn1_w_gate_up[0].astype(_BF), ffn1_w_down[0].astype(_BF))
    conv_p, pool_p, conv_s, pool_s, vn_s = [], [], [], [], []
    win_bf = None
    for layer in range(DEPTH):
        last = layer == DEPTH - 1
        fresh = mix_f32 if win_bf is None else mix_f32[1:]
        xp, xs, mix_bf = _ffn(xp, xs, row(ffn1_norm), *ffn1_bf, fnorm, layer, False, fresh, layer)
        if win_bf is not None:
            mix_bf = (win_bf,) + mix_bf
        nxt = ffn2_f32 if last else ffn2_f32 + (w_in,)
        xp, cp, pp, rounded = _mix_prompt(
            xp, layer, mixer_params(mix_bf, cw_slab, sgu_w, sb_full), nxt,
            [layer, layer] if last else [layer, layer, layer + 1])
        ffn2_bf, win_bf = rounded[:2], (None if last else rounded[2])
        xs, cs, ps, vn = _mix_sample(xs.reshape(DEC_SEQ, DEC_BATCH, D_MODEL), sc_t, sp_t, layer,
                                     mixer_params(mix_bf, conv_dw_w, sw4, sb4))
        xs = xs.reshape(DEC_SEQ * DEC_BATCH, D_MODEL)
        xp, xs, ffn1_bf = _ffn(xp, xs, row(ffn2_norm), *ffn2_bf, fnorm, layer, last,
                               () if last else ffn1_f32, layer + 1)
        conv_p.append(cp)
        pool_p.append(pp)
        conv_s.append(jnp.swapaxes(cs, 0, 1))
        pool_s.append(jnp.swapaxes(ps, 0, 1))
        vn_s.append(jnp.swapaxes(vn, 0, 1))

    y_prompt = xp.reshape(BATCH, SEQ, D_MODEL)
    y_sample = jnp.swapaxes(xs.reshape(DEC_SEQ, DEC_BATCH, D_MODEL), 0, 1)
    return (y_prompt, y_sample, jnp.stack(conv_p), jnp.stack(conv_s),
            jnp.stack(pool_p), jnp.stack(pool_s), jnp.stack(vn_s))
```
